```python
import math
import jax, jax.numpy as jnp
from jax import lax
import numpy as np

D_MODEL = 1024
BATCH = 2
SEQ = 16384
DEPTH = 2

CONV_DIM = 512
CONV_WIDTH = 31
GDN_HEADS = 4
GDN_DK = 128
GDN_DV = 128
GDN_QK = GDN_HEADS * GDN_DK
GDN_V = GDN_HEADS * GDN_DV
GDN_CONV = 4
GDN_CHUNK = 64
SGU_GROUPS = 4
SGU_GROUP_DIM = 128
SGU_DIM = SGU_GROUPS * SGU_GROUP_DIM
SGU_CHUNK = 128
D_FF = 4 * D_MODEL
DN_ALPHA = (2 * DEPTH) ** 0.25
DN_BETA = (8 * DEPTH) ** -0.25
LN_EPS = 1e-5
RMS_EPS = 1e-6
PROJ_SIZES = (2 * CONV_DIM, GDN_QK, GDN_QK, GDN_V, GDN_V, GDN_HEADS, GDN_HEADS, 2 * SGU_DIM, 3 * D_MODEL)
PROJ_COLS = sum(PROJ_SIZES)

kernel_name = "gated_parallel_conv_deltanet_sgu_deepnorm"


def layer_norm(x, g, b):
    xf = x.astype(jnp.float32)
    mu = jnp.mean(xf, -1, keepdims=True)
    var = jnp.mean(jnp.square(xf - mu), -1, keepdims=True)
    return ((xf - mu) * lax.rsqrt(var + LN_EPS) * g.astype(jnp.float32) + b.astype(jnp.float32)).astype(x.dtype)


def causal_depthwise_conv(x, w):
    width, ch = w.shape
    return lax.conv_general_dilated(
        x, w[:, None, :].astype(x.dtype), window_strides=(1,), padding=[(width - 1, 0)],
        dimension_numbers=("NWC", "WIO", "NWC"), feature_group_count=ch)


def l2norm(x):
    xf = x.astype(jnp.float32)
    return xf * lax.rsqrt(jnp.sum(xf * xf, -1, keepdims=True) + RMS_EPS)


def conformer_conv(a, w_dw, b_dw, ln_g, ln_b):
    a1, a2 = jnp.split(a, 2, axis=-1)
    h = a1 * jax.nn.sigmoid(a2)
    h = causal_depthwise_conv(h, w_dw) + b_dw
    h = layer_norm(h, ln_g, ln_b)
    return jax.nn.silu(h)


def chunked_gated_delta_rule(q, k, v, g, beta):
    bsz, seq, nh, dk = q.shape
    dv = v.shape[-1]
    c = GDN_CHUNK
    n = seq // c

    def chunks(t):
        return jnp.swapaxes(t.reshape(bsz, n, c, nh, *t.shape[3:]), 2, 3)

    q, k, v, g, beta = chunks(q), chunks(k), chunks(v), chunks(g), chunks(beta)
    gam = jnp.cumsum(g, axis=-1)
    causal = jnp.tril(jnp.ones((c, c), bool))
    strict = jnp.tril(jnp.ones((c, c), bool), -1)
    decay = jnp.exp(jnp.where(causal, gam[..., :, None] - gam[..., None, :], -jnp.inf))
    kk = jnp.einsum("bnhid,bnhjd->bnhij", k, k)
    lower = jnp.where(strict, beta[..., None] * kk * decay, 0.0)
    t_mat = lower + jnp.eye(c, dtype=lower.dtype)
    rhs = jnp.concatenate([beta[..., None] * v, beta[..., None] * k * jnp.exp(gam)[..., None]], -1)
    sol = lax.linalg.triangular_solve(t_mat, rhs, left_side=True, lower=True, unit_diagonal=True)
    u, w = sol[..., :dv], sol[..., dv:]
    a_qk = jnp.einsum("bnhid,bnhjd->bnhij", q, k) * decay
    q_dec = q * jnp.exp(gam)[..., None]
    g_last = gam[..., -1]
    k_dec = k * jnp.exp(g_last[..., None] - gam)[..., None]

    def step(state, inp):
        q_i, k_i, u_i, w_i, a_i, gl = inp
        v_new = u_i - jnp.einsum("bhcd,bhde->bhce", w_i, state)
        o = jnp.einsum("bhcd,bhde->bhce", q_i, state) + jnp.einsum("bhij,bhje->bhie", a_i, v_new)
        state = state * jnp.exp(gl)[..., None, None] + jnp.einsum("bhcd,bhce->bhde", k_i, v_new)
        return state, o

    xs = (jnp.moveaxis(q_dec, 1, 0), jnp.moveaxis(k_dec, 1, 0), jnp.moveaxis(u, 1, 0),
          jnp.moveaxis(w, 1, 0), jnp.moveaxis(a_qk, 1, 0), jnp.moveaxis(g_last, 1, 0))
    s0 = jnp.zeros((bsz, nh, dk, dv), jnp.float32)
    _, o = lax.scan(step, s0, xs)
    return jnp.swapaxes(jnp.moveaxis(o, 0, 1), 2, 3).reshape(bsz, seq, nh, dv)


def gated_deltanet(q, k, v, z, b_logit, a_logit, conv_q, conv_k, conv_v, a_log, dt_bias, norm_g):
    bsz, seq, _ = q.shape
    dtype = q.dtype
    q = jax.nn.silu(causal_depthwise_conv(q, conv_q)).reshape(bsz, seq, GDN_HEADS, GDN_DK)
    k = jax.nn.silu(causal_depthwise_conv(k, conv_k)).reshape(bsz, seq, GDN_HEADS, GDN_DK)
    v = jax.nn.silu(causal_depthwise_conv(v, conv_v)).reshape(bsz, seq, GDN_HEADS, GDN_DV)
    q = l2norm(q) * (GDN_DK ** -0.5)
    k = l2norm(k)
    beta = jax.nn.sigmoid(b_logit.astype(jnp.float32))
    g = -jnp.exp(a_log.astype(jnp.float32)) * jax.nn.softplus(
        a_logit.astype(jnp.float32) + dt_bias.astype(jnp.float32))
    o = chunked_gated_delta_rule(q, k, v.astype(jnp.float32), g, beta)
    zf = z.reshape(bsz, seq, GDN_HEADS, GDN_DV).astype(jnp.float32)
    o = o * lax.rsqrt(jnp.mean(o * o, -1, keepdims=True) + RMS_EPS) * norm_g.astype(jnp.float32)
    o = o * jax.nn.silu(zf)
    return o.reshape(bsz, seq, GDN_V).astype(dtype)


def spatial_gating(uv, ln_g, ln_b, w_s, b_s):
    bsz, seq, _ = uv.shape
    n = seq // SGU_CHUNK
    uv = jax.nn.gelu(uv)
    u, v = jnp.split(uv, 2, axis=-1)
    v = layer_norm(v, ln_g, ln_b)
    v = v.reshape(bsz, n, SGU_CHUNK, SGU_GROUPS, SGU_GROUP_DIM)
    w_causal = jnp.tril(w_s).astype(v.dtype)
    mixed = jnp.einsum("gpq,bnqgc->bnpgc", w_causal, v) + jnp.swapaxes(b_s, 0, 1)[None, None, :, :, None].astype(v.dtype)
    return u * mixed.reshape(bsz, seq, SGU_DIM)


def token_mixer(x, w_in, b_gate, conv_dw_w, conv_dw_b, conv_ln_g, conv_ln_b, w_pa,
                gdn_conv_q, gdn_conv_k, gdn_conv_v, gdn_a_log, gdn_dt_bias, gdn_norm_g, w_pb,
                sgu_ln_g, sgu_ln_b, sgu_w_s, sgu_b_s, w_pc, w_o):
    p = x @ w_in
    split_points = [int(s) for s in np.cumsum(PROJ_SIZES)[:-1]]
    a, q, k, v, z, b_logit, a_logit, uv, gate_logit = jnp.split(p, split_points, axis=-1)
    y_a = conformer_conv(a, conv_dw_w, conv_dw_b, conv_ln_g, conv_ln_b) @ w_pa
    y_b = gated_deltanet(q, k, v, z, b_logit, a_logit, gdn_conv_q, gdn_conv_k, gdn_conv_v,
                         gdn_a_log, gdn_dt_bias, gdn_norm_g) @ w_pb
    y_c = spatial_gating(uv, sgu_ln_g, sgu_ln_b, sgu_w_s, sgu_b_s) @ w_pc
    s_a, s_b, s_c = jnp.split(jax.nn.sigmoid(gate_logit + b_gate), 3, axis=-1)
    return (s_a * y_a + s_b * y_b + s_c * y_c) @ w_o


def setup_inputs(seed: int = 0) -> dict:
    key = jax.random.key(seed)
    ks = jax.random.split(key, 32)
    L = DEPTH
    f32 = jnp.float32

    def nrm(i, shape, scale):
        return jax.random.normal(ks[i], shape, f32) * scale

    dt = jnp.exp(jax.random.uniform(ks[14], (L, GDN_HEADS), f32) * (math.log(0.1) - math.log(0.001)) + math.log(0.001))
    return {
        "x": nrm(0, (BATCH, SEQ, D_MODEL), 1.0),
        "ln_in_g": 1.0 + nrm(1, (D_MODEL,), 0.02),
        "ln_in_b": nrm(2, (D_MODEL,), 0.02),
        "w_in": nrm(3, (L, D_MODEL, PROJ_COLS), D_MODEL ** -0.5),
        "b_gate": nrm(4, (L, 3 * D_MODEL), 0.1),
        "conv_dw_w": nrm(5, (L, CONV_WIDTH, CONV_DIM), CONV_WIDTH ** -0.5),
        "conv_dw_b": nrm(6, (L, CONV_DIM), 0.02),
        "conv_ln_g": 1.0 + nrm(7, (L, CONV_DIM), 0.02),
        "conv_ln_b": nrm(8, (L, CONV_DIM), 0.02),
        "w_pa": nrm(9, (L, CONV_DIM, D_MODEL), DN_BETA * CONV_DIM ** -0.5),
        "gdn_conv_q": nrm(10, (L, GDN_CONV, GDN_QK), GDN_CONV ** -0.5),
        "gdn_conv_k": nrm(11, (L, GDN_CONV, GDN_QK), GDN_CONV ** -0.5),
        "gdn_conv_v": nrm(12, (L, GDN_CONV, GDN_V), GDN_CONV ** -0.5),
        "gdn_a_log": jnp.log(jax.random.uniform(ks[13], (L, GDN_HEADS), f32, minval=1.0, maxval=16.0)),
        "gdn_dt_bias": dt + jnp.log(-jnp.expm1(-dt)),
        "gdn_norm_g": 1.0 + nrm(15, (L, GDN_DV), 0.02),
        "w_pb": nrm(16, (L, GDN_V, D_MODEL), DN_BETA * GDN_V ** -0.5),
        "sgu_ln_g": 1.0 + nrm(17, (L, SGU_DIM), 0.02),
        "sgu_ln_b": nrm(18, (L, SGU_DIM), 0.02),
        "sgu_w_s": nrm(19, (L, SGU_GROUPS, SGU_CHUNK, SGU_CHUNK), SGU_CHUNK ** -0.5),
        "sgu_b_s": 1.0 + nrm(20, (L, SGU_GROUPS, SGU_CHUNK), 0.02),
        "w_pc": nrm(21, (L, SGU_DIM, D_MODEL), DN_BETA * SGU_DIM ** -0.5),
        "w_o": nrm(22, (L, D_MODEL, D_MODEL), DN_BETA * D_MODEL ** -0.5),
        "ln1_g": 1.0 + nrm(23, (L, D_MODEL), 0.02),
        "ln1_b": nrm(24, (L, D_MODEL), 0.02),
        "w_ff1": nrm(25, (L, D_MODEL, D_FF), D_MODEL ** -0.5),
        "b_ff1": nrm(26, (L, D_FF), 0.02),
        "w_ff2": nrm(27, (L, D_FF, D_MODEL), DN_BETA * D_FF ** -0.5),
        "b_ff2": nrm(28, (L, D_MODEL), 0.02),
        "ln2_g": 1.0 + nrm(29, (L, D_MODEL), 0.02),
        "ln2_b": nrm(30, (L, D_MODEL), 0.02),
    }


def reference(x, ln_in_g, ln_in_b, w_in, b_gate, conv_dw_w, conv_dw_b, conv_ln_g, conv_ln_b, w_pa,
              gdn_conv_q, gdn_conv_k, gdn_conv_v, gdn_a_log, gdn_dt_bias, gdn_norm_g, w_pb,
              sgu_ln_g, sgu_ln_b, sgu_w_s, sgu_b_s, w_pc, w_o, ln1_g, ln1_b,
              w_ff1, b_ff1, w_ff2, b_ff2, ln2_g, ln2_b):
    x = layer_norm(x, ln_in_g, ln_in_b)
    for l in range(DEPTH):
        m = token_mixer(x, w_in[l], b_gate[l], conv_dw_w[l], conv_dw_b[l], conv_ln_g[l], conv_ln_b[l], w_pa[l],
                        gdn_conv_q[l], gdn_conv_k[l], gdn_conv_v[l], gdn_a_log[l], gdn_dt_bias[l], gdn_norm_g[l], w_pb[l],
                        sgu_ln_g[l], sgu_ln_b[l], sgu_w_s[l], sgu_b_s[l], w_pc[l], w_o[l])
        x = layer_norm(DN_ALPHA * x + m, ln1_g[l], ln1_b[l])
        h = jnp.square(jax.nn.relu(x @ w_ff1[l] + b_ff1[l]))
        x = layer_norm(DN_ALPHA * x + (h @ w_ff2[l] + b_ff2[l]), ln2_g[l], ln2_b[l])
    return x
```

```python
import functools
import math

import jax
import jax.numpy as jnp
from jax import lax
from jax.experimental import pallas as pl
from jax.experimental.pallas import tpu as pltpu

F32 = jnp.float32
BF16 = jnp.bfloat16

D_MODEL = 1024
DEPTH = 2
CONV_DIM = 512
CONV_WIDTH = 31
GDN_HEADS = 4
GDN_DK = 128
GDN_DV = 128
GDN_QK = GDN_HEADS * GDN_DK
GDN_V = GDN_HEADS * GDN_DV
GDN_CONV = 4
SGU_GROUPS = 4
SGU_GROUP_DIM = 128
SGU_DIM = SGU_GROUPS * SGU_GROUP_DIM
SGU_CHUNK = 128
D_FF = 4 * D_MODEL
DN_ALPHA = (2 * DEPTH) ** 0.25
LN_EPS = 1e-5
RMS_EPS = 1e-6

SUBLANES = 8
LANES = 128
VMEM_LIMIT_BYTES = 56 * 1024 * 1024

GDN_CHUNK = 64
GDN_TILE = 256
MIX_TILE = 256
FFN_TILE = 512
FFN_CHUNK = 1024
CONV_HIST = 32
GDN_CONV_HIST = 8


def _mm(a, b):
    return jnp.dot(a.astype(BF16), b.astype(BF16), preferred_element_type=F32)


def _mm_nt(a, b):
    return lax.dot_general(a.astype(BF16), b.astype(BF16), (((1,), (1,)), ((), ())),
                           preferred_element_type=F32)


def _mm_tn(a, b):
    return lax.dot_general(a.astype(BF16), b.astype(BF16), (((0,), (0,)), ((), ())),
                           preferred_element_type=F32)


def _split2(a):
    hi = a.astype(BF16)
    lo = (a - hi.astype(F32)).astype(BF16)
    return hi, lo


def _split3(a):
    hi = a.astype(BF16)
    r = a - hi.astype(F32)
    mid = r.astype(BF16)
    lo = (r - mid.astype(F32)).astype(BF16)
    return hi, mid, lo


def _mm_exact_lhs(a_bf16, b):
    out = None
    for part in _split3(b):
        term = jnp.dot(a_bf16, part, preferred_element_type=F32)
        out = term if out is None else out + term
    return out


def _mm_hi(a, b):
    a_hi, a_lo = _split2(a)
    b_hi, b_lo = _split2(b)
    return (jnp.dot(a_hi, b_hi, preferred_element_type=F32)
            + jnp.dot(a_hi, b_lo, preferred_element_type=F32)
            + jnp.dot(a_lo, b_hi, preferred_element_type=F32))


def _layer_norm(x, g, b):
    mu = jnp.mean(x, axis=-1, keepdims=True)
    xc = x - mu
    var = jnp.mean(xc * xc, axis=-1, keepdims=True)
    return xc * lax.rsqrt(var + LN_EPS) * g + b


def _sigmoid(x):
    return 1.0 / (1.0 + jnp.exp(-x))


def _silu(x):
    return x * _sigmoid(x)


def _gelu_tanh(x):
    return 0.5 * x * (1.0 + jnp.tanh(math.sqrt(2.0 / math.pi) * (x + 0.044715 * (x * x * x))))


def _softplus(x):
    return jnp.maximum(x, 0.0) + jnp.log1p(jnp.exp(-jnp.abs(x)))


def _unit_lower_inverse(lm, row, col):
    n = lm.shape[0]
    eye = (row == col).astype(F32)
    blk = (row // SUBLANES) == (col // SUBLANES)
    ld = jnp.where(blk, lm, 0.0)
    x = eye - ld
    p = _mm_hi(ld, ld)
    x = x + _mm_hi(x, p)
    p = _mm_hi(p, p)
    x = x + _mm_hi(x, p)
    size = SUBLANES
    while size < n:
        nxt = (row // (2 * size)) == (col // (2 * size))
        e = jnp.where(jnp.logical_and(nxt, jnp.logical_not(blk)), lm, 0.0)
        x = x - _mm_hi(x, _mm_hi(e, x))
        blk = nxt
        size *= 2
    return x


def _gdn_kernel(apply_ln_in, x_ref, lng_ref, lnb_ref, wqkv_ref, wz_ref, wba_ref, cw_ref, alog_ref, dtb_ref,
                ng_ref, o_ref, cbuf_ref, s_ref):
    ts = x_ref.shape[1]
    c = GDN_CHUNK
    t = pl.program_id(1)

    @pl.when(t == 0)
    def _():
        cbuf_ref[0:GDN_CONV_HIST, :] = jnp.zeros((GDN_CONV_HIST, cbuf_ref.shape[1]), F32)
        s_ref[...] = jnp.zeros(s_ref.shape, F32)

    x = x_ref[0]
    if apply_ln_in:
        x = _layer_norm(x, lng_ref[...], lnb_ref[...])
    xb = x.astype(BF16)

    pqkv = jnp.dot(xb, wqkv_ref[...], preferred_element_type=F32)
    cbuf_ref[GDN_CONV_HIST:GDN_CONV_HIST + ts, :] = pqkv
    conv = cw_ref[GDN_CONV - 1:GDN_CONV, :] * pqkv
    for j in range(GDN_CONV - 1):
        off = GDN_CONV_HIST - (GDN_CONV - 1) + j
        conv = conv + cw_ref[j:j + 1, :] * cbuf_ref[off:off + ts, :]
    cbuf_ref[0:GDN_CONV_HIST, :] = cbuf_ref[ts:ts + GDN_CONV_HIST, :]
    qkv = _silu(conv)

    ba = jnp.dot(xb, wba_ref[...], preferred_element_type=F32)
    beta_all = _sigmoid(ba)
    g_all = -jnp.exp(alog_ref[...]) * _softplus(ba + dtb_ref[...])
    z = jnp.dot(xb, wz_ref[...], preferred_element_type=F32)

    row = lax.broadcasted_iota(jnp.int32, (c, c), 0)
    col = lax.broadcasted_iota(jnp.int32, (c, c), 1)
    ge = row >= col
    gt = row > col
    ltri = ge.astype(BF16)
    ones = jnp.ones((c, c), BF16)

    for h in range(GDN_HEADS):
        q = qkv[:, h * GDN_DK:(h + 1) * GDN_DK]
        k = qkv[:, GDN_QK + h * GDN_DK:GDN_QK + (h + 1) * GDN_DK]
        v = qkv[:, 2 * GDN_QK + h * GDN_DV:2 * GDN_QK + (h + 1) * GDN_DV]
        q = q * lax.rsqrt(jnp.sum(q * q, axis=-1, keepdims=True) + RMS_EPS) * (GDN_DK ** -0.5)
        k = k * lax.rsqrt(jnp.sum(k * k, axis=-1, keepdims=True) + RMS_EPS)
        zh = z[:, h * GDN_DV:(h + 1) * GDN_DV]
        for ci in range(ts // c):
            r0 = ci * c
            qc, kc, vc = q[r0:r0 + c], k[r0:r0 + c], v[r0:r0 + c]
            bc = jnp.broadcast_to(beta_all[r0:r0 + c, h:h + 1], (c, LANES))
            gc = jnp.broadcast_to(g_all[r0:r0 + c, GDN_HEADS + h:GDN_HEADS + h + 1], (c, LANES))
            gam = _mm_exact_lhs(ltri, gc)
            gam_row = _mm_exact_lhs(ones, jnp.where(row <= col, gc[:, :c], 0.0))
            decay = jnp.where(ge, jnp.exp(jnp.minimum(gam[:, :c] - gam_row, 0.0)), 0.0)
            egam = jnp.exp(gam)
            kk = _mm_nt(kc, kc)
            lm = jnp.where(gt, bc[:, :c] * kk * decay, 0.0)
            tinv = _unit_lower_inverse(lm, row, col)
            rhs = jnp.concatenate([bc * vc, bc * kc * egam], axis=1)
            uw = _mm_hi(tinv, rhs)
            u, w = uw[:, :GDN_DV], uw[:, GDN_DV:]
            aqk = jnp.where(ge, _mm_nt(qc, kc) * decay, 0.0)
            qd = qc * egam
            glast = gam[c - 1:c, :]
            kd = kc * jnp.exp(glast - gam)
            s = s_ref[h]
            v_new = u - _mm(w, s)
            o = _mm(qd, s) + _mm(aqk, v_new)
            s_ref[h] = s * jnp.exp(glast) + _mm_tn(kd, v_new)
            o = o * lax.rsqrt(jnp.mean(o * o, axis=-1, keepdims=True) + RMS_EPS) * ng_ref[...]
            o_ref[0, r0:r0 + c, h * GDN_DV:(h + 1) * GDN_DV] = o * _silu(zh[r0:r0 + c])


def _mix_kernel(apply_ln_in, x_ref, og_ref, lng_ref, lnb_ref, wa_ref, wuv_ref, wg_ref, bg_ref, cw_ref, cb_ref,
                clg_ref, clb_ref, pa_ref, pb_ref, slg_ref, slb_ref, ws_ref, bst_ref, pc_ref, wo_ref,
                l1g_ref, l1b_ref, o_ref, hbuf_ref):
    ts = x_ref.shape[1]
    t = pl.program_id(1)

    @pl.when(t == 0)
    def _():
        hbuf_ref[0:CONV_HIST, :] = jnp.zeros((CONV_HIST, CONV_DIM), F32)

    x = x_ref[0]
    if apply_ln_in:
        x = _layer_norm(x, lng_ref[...], lnb_ref[...])
    xb = x.astype(BF16)

    pa = jnp.dot(xb, wa_ref[...], preferred_element_type=F32)
    hbuf_ref[CONV_HIST:CONV_HIST + ts, :] = pa[:, :CONV_DIM] * _sigmoid(pa[:, CONV_DIM:])
    conv = jnp.broadcast_to(cb_ref[...], (ts, CONV_DIM))
    for j in range(CONV_WIDTH):
        off = CONV_HIST - (CONV_WIDTH - 1) + j
        conv = conv + cw_ref[j:j + 1, :] * hbuf_ref[off:off + ts, :]
    hbuf_ref[0:CONV_HIST, :] = hbuf_ref[ts:ts + CONV_HIST, :]
    ya = _mm(_silu(_layer_norm(conv, clg_ref[...], clb_ref[...])), pa_ref[...])
    merged = _sigmoid(jnp.dot(xb, wg_ref[:, 0:D_MODEL], preferred_element_type=F32) + bg_ref[:, 0:D_MODEL]) * ya

    yb = _mm(og_ref[0], pb_ref[...])
    merged = merged + _sigmoid(jnp.dot(xb, wg_ref[:, D_MODEL:2 * D_MODEL], preferred_element_type=F32)
                               + bg_ref[:, D_MODEL:2 * D_MODEL]) * yb

    guv = _gelu_tanh(jnp.dot(xb, wuv_ref[...], preferred_element_type=F32))
    u = guv[:, :SGU_DIM]
    vb = _layer_norm(guv[:, SGU_DIM:], slg_ref[...], slb_ref[...]).astype(BF16)
    prow = lax.broadcasted_iota(jnp.int32, (SGU_CHUNK, SGU_CHUNK), 0)
    pcol = lax.broadcasted_iota(jnp.int32, (SGU_CHUNK, SGU_CHUNK), 1)
    w_causal = [jnp.where(prow >= pcol, ws_ref[g], 0.0).astype(BF16) for g in range(SGU_GROUPS)]
    chunks = []
    for ci in range(ts // SGU_CHUNK):
        r0 = ci * SGU_CHUNK
        groups = []
        for g in range(SGU_GROUPS):
            vg = vb[r0:r0 + SGU_CHUNK, g * SGU_GROUP_DIM:(g + 1) * SGU_GROUP_DIM]
            groups.append(jnp.dot(w_causal[g], vg, preferred_element_type=F32) + bst_ref[:, g:g + 1])
        chunks.append(jnp.concatenate(groups, axis=1))
    mixed = jnp.concatenate(chunks, axis=0)
    yc = _mm(u * mixed, pc_ref[...])
    merged = merged + _sigmoid(jnp.dot(xb, wg_ref[:, 2 * D_MODEL:], preferred_element_type=F32)
                               + bg_ref[:, 2 * D_MODEL:]) * yc

    m = _mm(merged, wo_ref[...])
    o_ref[0] = _layer_norm(DN_ALPHA * x + m, l1g_ref[...], l1b_ref[...])


def _ffn_kernel(x_ref, w1_ref, b1_ref, w2_ref, b2_ref, g_ref, b_ref, o_ref):
    x = x_ref[0]
    xb = x.astype(BF16)
    acc = None
    for ci in range(D_FF // FFN_CHUNK):
        lo, hi = ci * FFN_CHUNK, (ci + 1) * FFN_CHUNK
        h = jnp.dot(xb, w1_ref[:, lo:hi], preferred_element_type=F32) + b1_ref[:, lo:hi]
        h = jnp.square(jnp.maximum(h, 0.0))
        part = jnp.dot(h.astype(BF16), w2_ref[lo:hi, :], preferred_element_type=F32)
        acc = part if acc is None else acc + part
    o_ref[0] = _layer_norm(DN_ALPHA * x + (acc + b2_ref[...]), g_ref[...], b_ref[...])


def _full_spec(arr):
    zeros = (0,) * arr.ndim
    return pl.BlockSpec(arr.shape, lambda b, t: zeros)


def _tile_spec(tile, width):
    return pl.BlockSpec((1, tile, width), lambda b, t: (b, t, 0))


def _compiler_params():
    return pltpu.CompilerParams(dimension_semantics=("arbitrary", "arbitrary"),
                                vmem_limit_bytes=VMEM_LIMIT_BYTES)


def _row(v):
    return v.reshape(1, -1).astype(F32)


def _gdn_call(x, apply_ln_in, ln_in_g, ln_in_b, wqkv, wz, wba, cw, alog, dtb, ng):
    bsz, seq, _ = x.shape
    tile = min(GDN_TILE, seq)
    params = (ln_in_g, ln_in_b, wqkv, wz, wba, cw, alog, dtb, ng)
    return pl.pallas_call(
        functools.partial(_gdn_kernel, apply_ln_in),
        grid=(bsz, seq // tile),
        in_specs=[_tile_spec(tile, D_MODEL)] + [_full_spec(p) for p in params],
        out_specs=_tile_spec(tile, GDN_V),
        out_shape=jax.ShapeDtypeStruct((bsz, seq, GDN_V), F32),
        scratch_shapes=[pltpu.VMEM((tile + GDN_CONV_HIST, 2 * GDN_QK + GDN_V), F32),
                        pltpu.VMEM((GDN_HEADS, GDN_DK, GDN_DV), F32)],
        compiler_params=_compiler_params(),
        name="gdn",
    )(x, *params)


def _mix_call(x, og, apply_ln_in, params):
    bsz, seq, _ = x.shape
    tile = min(MIX_TILE, seq)
    return pl.pallas_call(
        functools.partial(_mix_kernel, apply_ln_in),
        grid=(bsz, seq // tile),
        in_specs=[_tile_spec(tile, D_MODEL), _tile_spec(tile, GDN_V)] + [_full_spec(p) for p in params],
        out_specs=_tile_spec(tile, D_MODEL),
        out_shape=jax.ShapeDtypeStruct((bsz, seq, D_MODEL), F32),
        scratch_shapes=[pltpu.VMEM((tile + CONV_HIST, CONV_DIM), F32)],
        compiler_params=_compiler_params(),
        name="mix",
    )(x, og, *params)


def _ffn_call(x, params):
    bsz, seq, _ = x.shape
    tile = min(FFN_TILE, seq)
    return pl.pallas_call(
        _ffn_kernel,
        grid=(bsz, seq // tile),
        in_specs=[_tile_spec(tile, D_MODEL)] + [_full_spec(p) for p in params],
        out_specs=_tile_spec(tile, D_MODEL),
        out_shape=jax.ShapeDtypeStruct((bsz, seq, D_MODEL), F32),
        compiler_params=_compiler_params(),
        name="ffn",
    )(x, *params)


def kernel(x, ln_in_g, ln_in_b, w_in, b_gate, conv_dw_w, conv_dw_b, conv_ln_g, conv_ln_b, w_pa, gdn_conv_q, gdn_conv_k, gdn_conv_v, gdn_a_log, gdn_dt_bias, gdn_norm_g, w_pb, sgu_ln_g, sgu_ln_b, sgu_w_s, sgu_b_s, w_pc, w_o, ln1_g, ln1_b, w_ff1, b_ff1, w_ff2, b_ff2, ln2_g, ln2_b):
    o_a = 0
    o_q = o_a + 2 * CONV_DIM
    o_z = o_q + 2 * GDN_QK + GDN_V
    o_b = o_z + GDN_V
    o_uv = o_b + 2 * GDN_HEADS
    o_g = o_uv + 2 * SGU_DIM
    head_pad = LANES - 2 * GDN_HEADS
    lng, lnb = _row(ln_in_g), _row(ln_in_b)
    for l in range(DEPTH):
        w = w_in[l]
        wa = w[:, o_a:o_q].astype(BF16)
        wqkv = w[:, o_q:o_z].astype(BF16)
        wz = w[:, o_z:o_b].astype(BF16)
        wba = jnp.pad(w[:, o_b:o_uv], ((0, 0), (0, head_pad))).astype(BF16)
        wuv = w[:, o_uv:o_g].astype(BF16)
        wg = w[:, o_g:].astype(BF16)
        cw_qkv = jnp.concatenate([gdn_conv_q[l], gdn_conv_k[l], gdn_conv_v[l]], axis=1)
        alog = jnp.pad(gdn_a_log[l], (GDN_HEADS, head_pad)).reshape(1, LANES)
        dtb = jnp.pad(gdn_dt_bias[l], (GDN_HEADS, head_pad)).reshape(1, LANES)
        first = l == 0
        og = _gdn_call(x, first, lng, lnb, wqkv, wz, wba, cw_qkv, alog, dtb, _row(gdn_norm_g[l]))
        mix_params = (lng, lnb, wa, wuv, wg, _row(b_gate[l]), conv_dw_w[l], _row(conv_dw_b[l]),
                      _row(conv_ln_g[l]), _row(conv_ln_b[l]), w_pa[l].astype(BF16), w_pb[l].astype(BF16),
                      _row(sgu_ln_g[l]), _row(sgu_ln_b[l]), sgu_w_s[l], sgu_b_s[l].T, w_pc[l].astype(BF16),
                      w_o[l].astype(BF16), _row(ln1_g[l]), _row(ln1_b[l]))
        x = _mix_call(x, og, first, mix_params)
        ffn_params = (w_ff1[l].astype(BF16), _row(b_ff1[l]), w_ff2[l].astype(BF16), _row(b_ff2[l]),
                      _row(ln2_g[l]), _row(ln2_b[l]))
        x = _ffn_call(x, ffn_params)
    return x
```

```python
import functools
import math

import jax
import jax.numpy as jnp
from jax import lax
from jax.experimental import pallas as pl
from jax.experimental.pallas import tpu as pltpu

F32 = jnp.float32
BF16 = jnp.bfloat16

D_MODEL = 1024
DEPTH = 2
CONV_DIM = 512
CONV_WIDTH = 31
GDN_HEADS = 4
GDN_DK = 128
GDN_DV = 128
GDN_QK = GDN_HEADS * GDN_DK
GDN_V = GDN_HEADS * GDN_DV
GDN_CONV = 4
SGU_GROUPS = 4
SGU_GROUP_DIM = 128
SGU_DIM = SGU_GROUPS * SGU_GROUP_DIM
SGU_CHUNK = 128
D_FF = 4 * D_MODEL
DN_ALPHA = (2 * DEPTH) ** 0.25
LN_EPS = 1e-5
RMS_EPS = 1e-6

SUBLANES = 8
LANES = 128
VMEM_LIMIT_BYTES = 56 * 1024 * 1024

GDN_TILE = 256
MIX_TILE = 256
FFN_TILE = 512
FFN_CHUNK = 1024
CONV_HIST = 32
GDN_CONV_HIST = 8


def _mm(a, b):
    return jnp.dot(a.astype(BF16), b.astype(BF16), preferred_element_type=F32)


def _mm_nt(a, b):
    return lax.dot_general(a.astype(BF16), b.astype(BF16), (((1,), (1,)), ((), ())),
                           preferred_element_type=F32)


def _mm_tn(a, b):
    return lax.dot_general(a.astype(BF16), b.astype(BF16), (((0,), (0,)), ((), ())),
                           preferred_element_type=F32)


def _split2(a):
    hi = a.astype(BF16)
    lo = (a - hi.astype(F32)).astype(BF16)
    return hi, lo


def _split3(a):
    hi = a.astype(BF16)
    r = a - hi.astype(F32)
    mid = r.astype(BF16)
    lo = (r - mid.astype(F32)).astype(BF16)
    return hi, mid, lo


def _mm_exact_lhs(a_bf16, b):
    out = None
    for part in _split3(b):
        term = jnp.dot(a_bf16, part, preferred_element_type=F32)
        out = term if out is None else out + term
    return out


def _layer_norm(x, g, b):
    mu = jnp.mean(x, axis=-1, keepdims=True)
    xc = x - mu
    var = jnp.mean(xc * xc, axis=-1, keepdims=True)
    return xc * lax.rsqrt(var + LN_EPS) * g + b


def _sigmoid(x):
    return 1.0 / (1.0 + jnp.exp(-x))


def _silu(x):
    return x * _sigmoid(x)


def _gelu_tanh(x):
    return 0.5 * x * (1.0 + jnp.tanh(math.sqrt(2.0 / math.pi) * (x + 0.044715 * (x * x * x))))


def _softplus(x):
    return jnp.maximum(x, 0.0) + jnp.log1p(jnp.exp(-jnp.abs(x)))


def _bdot(a, b):
    return jnp.dot(a, b, preferred_element_type=F32)


def _unit_lower_inverses(lms, row, col):
    n = lms[0].shape[0]
    dist = row ^ col
    eye = (row == col).astype(F32)
    lds = [jnp.where(dist < SUBLANES, lm, 0.0) for lm in lms]
    ldb = [ld.astype(BF16) for ld in lds]
    xs = [eye - ld for ld in lds]
    ps = [_bdot(b, b) for b in ldb]
    pb = [p.astype(BF16) for p in ps]
    xs = [x + _bdot(x.astype(BF16), p) for x, p in zip(xs, pb)]
    pb = [_bdot(p, p).astype(BF16) for p in pb]
    xs = [x + _bdot(x.astype(BF16), p) for x, p in zip(xs, pb)]
    size = SUBLANES
    while size < n:
        band = jnp.logical_and(dist >= size, dist < 2 * size)
        eb = [jnp.where(band, lm, 0.0).astype(BF16) for lm in lms]
        xb = [x.astype(BF16) for x in xs]
        ys = [_bdot(e, x).astype(BF16) for e, x in zip(eb, xb)]
        xs = [x - _bdot(b, y) for x, b, y in zip(xs, xb, ys)]
        size *= 2
    return xs


def _gdn_kernel(apply_ln_in, x_ref, lng_ref, lnb_ref, wqkv_ref, wz_ref, wba_ref, cw_ref, alog_ref, dtb_ref,
                ng_ref, o_ref, cbuf_ref, s_ref):
    ts = x_ref.shape[1]
    t = pl.program_id(1)

    @pl.when(t == 0)
    def _():
        cbuf_ref[0:GDN_CONV_HIST, :] = jnp.zeros((GDN_CONV_HIST, cbuf_ref.shape[1]), F32)
        s_ref[...] = jnp.zeros(s_ref.shape, F32)

    x = x_ref[0]
    if apply_ln_in:
        x = _layer_norm(x, lng_ref[...], lnb_ref[...])
    xb = x.astype(BF16)

    pqkv = jnp.dot(xb, wqkv_ref[...], preferred_element_type=F32)
    cbuf_ref[GDN_CONV_HIST:GDN_CONV_HIST + ts, :] = pqkv
    conv = cw_ref[GDN_CONV - 1:GDN_CONV, :] * pqkv
    for j in range(GDN_CONV - 1):
        off = GDN_CONV_HIST - (GDN_CONV - 1) + j
        conv = conv + cw_ref[j:j + 1, :] * cbuf_ref[off:off + ts, :]
    cbuf_ref[0:GDN_CONV_HIST, :] = cbuf_ref[ts:ts + GDN_CONV_HIST, :]
    qkv = _silu(conv)

    ba = jnp.dot(xb, wba_ref[...], preferred_element_type=F32)
    beta_all = _sigmoid(ba)
    g_all = -jnp.exp(alog_ref[...]) * _softplus(ba + dtb_ref[...])
    z = jnp.dot(xb, wz_ref[...], preferred_element_type=F32)

    row = lax.broadcasted_iota(jnp.int32, (ts, ts), 0)
    col = lax.broadcasted_iota(jnp.int32, (ts, ts), 1)
    ge = row >= col
    gt = row > col
    gam_all = _mm_exact_lhs(ge.astype(BF16), g_all)
    gam_all_t = gam_all.T

    heads = range(GDN_HEADS)
    qs, ks, vs, betas, gams, decays, lms = [], [], [], [], [], [], []
    for h in heads:
        q = qkv[:, h * GDN_DK:(h + 1) * GDN_DK]
        k = qkv[:, GDN_QK + h * GDN_DK:GDN_QK + (h + 1) * GDN_DK]
        qs.append(q * lax.rsqrt(jnp.sum(q * q, axis=-1, keepdims=True) + RMS_EPS) * (GDN_DK ** -0.5))
        ks.append(k * lax.rsqrt(jnp.sum(k * k, axis=-1, keepdims=True) + RMS_EPS))
        vs.append(qkv[:, 2 * GDN_QK + h * GDN_DV:2 * GDN_QK + (h + 1) * GDN_DV])
        betas.append(jnp.broadcast_to(beta_all[:, h:h + 1], (ts, LANES)))
        gams.append(jnp.broadcast_to(gam_all[:, GDN_HEADS + h:GDN_HEADS + h + 1], (ts, LANES)))
    for h in heads:
        gam_i = jnp.concatenate([gams[h]] * (ts // LANES), axis=1)
        gam_j = jnp.broadcast_to(gam_all_t[GDN_HEADS + h:GDN_HEADS + h + 1, :], (ts, ts))
        decays.append(jnp.where(ge, jnp.exp(jnp.minimum(gam_i - gam_j, 0.0)), 0.0))
        beta_i = jnp.concatenate([betas[h]] * (ts // LANES), axis=1)
        lms.append(jnp.where(gt, beta_i * _mm_nt(ks[h], ks[h]) * decays[h], 0.0))
    tinvs = _unit_lower_inverses(lms, row, col)
    egams = [jnp.exp(g) for g in gams]
    uws = [_mm(tinvs[h], jnp.concatenate([betas[h] * vs[h], betas[h] * ks[h] * egams[h]], axis=1))
           for h in heads]
    aqks = [jnp.where(ge, _mm_nt(qs[h], ks[h]) * decays[h], 0.0) for h in heads]
    glasts = [g[ts - 1:ts, :] for g in gams]
    ss = [s_ref[h] for h in heads]
    v_news = [uws[h][:, :GDN_DV] - _mm(uws[h][:, GDN_DV:], ss[h]) for h in heads]
    for h in heads:
        s_ref[h] = ss[h] * jnp.exp(glasts[h]) + _mm_tn(ks[h] * jnp.exp(glasts[h] - gams[h]), v_news[h])
    for h in heads:
        o = _mm(qs[h] * egams[h], ss[h]) + _mm(aqks[h], v_news[h])
        o = o * lax.rsqrt(jnp.mean(o * o, axis=-1, keepdims=True) + RMS_EPS) * ng_ref[...]
        o_ref[0, :, h * GDN_DV:(h + 1) * GDN_DV] = o * _silu(z[:, h * GDN_DV:(h + 1) * GDN_DV])


def _mix_kernel(apply_ln_in, x_ref, og_ref, lng_ref, lnb_ref, wa_ref, wuv_ref, wg_ref, bg_ref, cw_ref, cb_ref,
                clg_ref, clb_ref, pa_ref, pb_ref, slg_ref, slb_ref, ws_ref, bst_ref, pc_ref, wo_ref,
                l1g_ref, l1b_ref, o_ref, hbuf_ref):
    ts = x_ref.shape[1]
    t = pl.program_id(1)

    @pl.when(t == 0)
    def _():
        hbuf_ref[0:CONV_HIST, :] = jnp.zeros((CONV_HIST, CONV_DIM), F32)

    x = x_ref[0]
    if apply_ln_in:
        x = _layer_norm(x, lng_ref[...], lnb_ref[...])
    xb = x.astype(BF16)

    pa = jnp.dot(xb, wa_ref[...], preferred_element_type=F32)
    hbuf_ref[CONV_HIST:CONV_HIST + ts, :] = pa[:, :CONV_DIM] * _sigmoid(pa[:, CONV_DIM:])
    first = CONV_HIST - (CONV_WIDTH - 1)
    conv = jnp.broadcast_to(cb_ref[...], (ts, CONV_DIM))
    for s in range(SUBLANES):
        taps = [j for j in range(CONV_WIDTH) if (first + j) % SUBLANES == s]
        rows = ts if s == 0 else ts + SUBLANES
        part = None
        for j in taps:
            base = first + j - s
            term = cw_ref[j:j + 1, :] * hbuf_ref[base:base + rows, :]
            part = term if part is None else part + term
        conv = conv + part[s:s + ts, :]
    hbuf_ref[0:CONV_HIST, :] = hbuf_ref[ts:ts + CONV_HIST, :]
    ya = _mm(_silu(_layer_norm(conv, clg_ref[...], clb_ref[...])), pa_ref[...])
    merged = _sigmoid(jnp.dot(xb, wg_ref[:, 0:D_MODEL], preferred_element_type=F32) + bg_ref[:, 0:D_MODEL]) * ya

    yb = _mm(og_ref[0], pb_ref[...])
    merged = merged + _sigmoid(jnp.dot(xb, wg_ref[:, D_MODEL:2 * D_MODEL], preferred_element_type=F32)
                               + bg_ref[:, D_MODEL:2 * D_MODEL]) * yb

    guv = _gelu_tanh(jnp.dot(xb, wuv_ref[...], preferred_element_type=F32))
    u = guv[:, :SGU_DIM]
    vb = _layer_norm(guv[:, SGU_DIM:], slg_ref[...], slb_ref[...]).astype(BF16)
    prow = lax.broadcasted_iota(jnp.int32, (SGU_CHUNK, SGU_CHUNK), 0)
    pcol = lax.broadcasted_iota(jnp.int32, (SGU_CHUNK, SGU_CHUNK), 1)
    w_causal = [jnp.where(prow >= pcol, ws_ref[g], 0.0).astype(BF16) for g in range(SGU_GROUPS)]
    chunks = []
    for ci in range(ts // SGU_CHUNK):
        r0 = ci * SGU_CHUNK
        groups = []
        for g in range(SGU_GROUPS):
            vg = vb[r0:r0 + SGU_CHUNK, g * SGU_GROUP_DIM:(g + 1) * SGU_GROUP_DIM]
            groups.append(jnp.dot(w_causal[g], vg, preferred_element_type=F32) + bst_ref[:, g:g + 1])
        chunks.append(jnp.concatenate(groups, axis=1))
    mixed = jnp.concatenate(chunks, axis=0)
    yc = _mm(u * mixed, pc_ref[...])
    merged = merged + _sigmoid(jnp.dot(xb, wg_ref[:, 2 * D_MODEL:], preferred_element_type=F32)
                               + bg_ref[:, 2 * D_MODEL:]) * yc

    m = _mm(merged, wo_ref[...])
    o_ref[0] = _layer_norm(DN_ALPHA * x + m, l1g_ref[...], l1b_ref[...])


def _ffn_kernel(x_ref, w1_ref, b1_ref, w2_ref, b2_ref, g_ref, b_ref, o_ref):
    x = x_ref[0]
    xb = x.astype(BF16)
    acc = None
    for ci in range(D_FF // FFN_CHUNK):
        lo, hi = ci * FFN_CHUNK, (ci + 1) * FFN_CHUNK
        h = jnp.dot(xb, w1_ref[:, lo:hi], preferred_element_type=F32) + b1_ref[:, lo:hi]
        h = jnp.square(jnp.maximum(h, 0.0))
        part = jnp.dot(h.astype(BF16), w2_ref[lo:hi, :], preferred_element_type=F32)
        acc = part if acc is None else acc + part
    o_ref[0] = _layer_norm(DN_ALPHA * x + (acc + b2_ref[...]), g_ref[...], b_ref[...])


def _full_spec(arr):
    zeros = (0,) * arr.ndim
    return pl.BlockSpec(arr.shape, lambda b, t: zeros)


def _tile_spec(tile, width):
    return pl.BlockSpec((1, tile, width), lambda b, t: (b, t, 0))


def _compiler_params():
    return pltpu.CompilerParams(dimension_semantics=("arbitrary", "arbitrary"),
                                vmem_limit_bytes=VMEM_LIMIT_BYTES)


def _row(v):
    return v.reshape(1, -1).astype(F32)


def _gdn_call(x, apply_ln_in, ln_in_g, ln_in_b, wqkv, wz, wba, cw, alog, dtb, ng):
    bsz, seq, _ = x.shape
    tile = min(GDN_TILE, seq)
    params = (ln_in_g, ln_in_b, wqkv, wz, wba, cw, alog, dtb, ng)
    return pl.pallas_call(
        functools.partial(_gdn_kernel, apply_ln_in),
        grid=(bsz, seq // tile),
        in_specs=[_tile_spec(tile, D_MODEL)] + [_full_spec(p) for p in params],
        out_specs=_tile_spec(tile, GDN_V),
        out_shape=jax.ShapeDtypeStruct((bsz, seq, GDN_V), F32),
        scratch_shapes=[pltpu.VMEM((tile + GDN_CONV_HIST, 2 * GDN_QK + GDN_V), F32),
                        pltpu.VMEM((GDN_HEADS, GDN_DK, GDN_DV), F32)],
        compiler_params=_compiler_params(),
        name="gdn",
    )(x, *params)


def _mix_call(x, og, apply_ln_in, params):
    bsz, seq, _ = x.shape
    tile = min(MIX_TILE, seq)
    return pl.pallas_call(
        functools.partial(_mix_kernel, apply_ln_in),
        grid=(bsz, seq // tile),
        in_specs=[_tile_spec(tile, D_MODEL), _tile_spec(tile, GDN_V)] + [_full_spec(p) for p in params],
        out_specs=_tile_spec(tile, D_MODEL),
        out_shape=jax.ShapeDtypeStruct((bsz, seq, D_MODEL), F32),
        scratch_shapes=[pltpu.VMEM((tile + CONV_HIST, CONV_DIM), F32)],
        compiler_params=_compiler_params(),
        name="mix",
    )(x, og, *params)


def _ffn_call(x, params):
    bsz, seq, _ = x.shape
    tile = min(FFN_TILE, seq)
    return pl.pallas_call(
        _ffn_kernel,
        grid=(bsz, seq // tile),
        in_specs=[_tile_spec(tile, D_MODEL)] + [_full_spec(p) for p in params],
        out_specs=_tile_spec(tile, D_MODEL),
        out_shape=jax.ShapeDtypeStruct((bsz, seq, D_MODEL), F32),
        compiler_params=_compiler_params(),
        name="ffn",
    )(x, *params)


def kernel(x, ln_in_g, ln_in_b, w_in, b_gate, conv_dw_w, conv_dw_b, conv_ln_g, conv_ln_b, w_pa, gdn_conv_q, gdn_conv_k, gdn_conv_v, gdn_a_log, gdn_dt_bias, gdn_norm_g, w_pb, sgu_ln_g, sgu_ln_b, sgu_w_s, sgu_b_s, w_pc, w_o, ln1_g, ln1_b, w_ff1, b_ff1, w_ff2, b_ff2, ln2_g, ln2_b):
    o_a = 0
    o_q = o_a + 2 * CONV_DIM
    o_z = o_q + 2 * GDN_QK + GDN_V
    o_b = o_z + GDN_V
    o_uv = o_b + 2 * GDN_HEADS
    o_g = o_uv + 2 * SGU_DIM
    head_pad = LANES - 2 * GDN_HEADS
    lng, lnb = _row(ln_in_g), _row(ln_in_b)
    for l in range(DEPTH):
        w = w_in[l]
        wa = w[:, o_a:o_q].astype(BF16)
        wqkv = w[:, o_q:o_z].astype(BF16)
        wz = w[:, o_z:o_b].astype(BF16)
        wba = jnp.pad(w[:, o_b:o_uv], ((0, 0), (0, head_pad))).astype(BF16)
        wuv = w[:, o_uv:o_g].astype(BF16)
        wg = w[:, o_g:].astype(BF16)
        cw_qkv = jnp.concatenate([gdn_conv_q[l], gdn_conv_k[l], gdn_conv_v[l]], axis=1)
        alog = jnp.pad(gdn_a_log[l], (GDN_HEADS, head_pad)).reshape(1, LANES)
        dtb = jnp.pad(gdn_dt_bias[l], (GDN_HEADS, head_pad)).reshape(1, LANES)
        first = l == 0
        og = _gdn_call(x, first, lng, lnb, wqkv, wz, wba, cw_qkv, alog, dtb, _row(gdn_norm_g[l]))
        mix_params = (lng, lnb, wa, wuv, wg, _row(b_gate[l]), conv_dw_w[l], _row(conv_dw_b[l]),
                      _row(conv_ln_g[l]), _row(conv_ln_b[l]), w_pa[l].astype(BF16), w_pb[l].astype(BF16),
                      _row(sgu_ln_g[l]), _row(sgu_ln_b[l]), sgu_w_s[l], sgu_b_s[l].T, w_pc[l].astype(BF16),
                      w_o[l].astype(BF16), _row(ln1_g[l]), _row(ln1_b[l]))
        x = _mix_call(x, og, first, mix_params)
        ffn_params = (w_ff1[l].astype(BF16), _row(b_ff1[l]), w_ff2[l].astype(BF16), _row(b_ff2[l]),
                      _row(ln2_g[l]), _row(ln2_b[l]))
        x = _ffn_call(x, ffn_params)
    return x
```

```python
import functools
import math

import jax
import jax.numpy as jnp
from jax import lax
from jax.experimental import pallas as pl
from jax.experimental.pallas import tpu as pltpu

F32 = jnp.float32
BF16 = jnp.bfloat16

D_MODEL = 1024
DEPTH = 2
CONV_DIM = 512
CONV_WIDTH = 31
GDN_HEADS = 4
GDN_DK = 128
GDN_DV = 128
GDN_QK = GDN_HEADS * GDN_DK
GDN_V = GDN_HEADS * GDN_DV
GDN_CONV = 4
SGU_GROUPS = 4
SGU_GROUP_DIM = 128
SGU_DIM = SGU_GROUPS * SGU_GROUP_DIM
SGU_CHUNK = 128
D_FF = 4 * D_MODEL
DN_ALPHA = (2 * DEPTH) ** 0.25
LN_EPS = 1e-5
RMS_EPS = 1e-6

SUBLANES = 8
LANES = 128
VMEM_LIMIT_BYTES = 56 * 1024 * 1024

MIX_TILE = 256
FFN_TILE = 512
FFN_CHUNK = 1024
MIX_COLS = 256
CONV_HIST = 32
GDN_CONV_HIST = 8


def _bdot(a, b):
    return jnp.dot(a, b, preferred_element_type=F32)


def _mm(a, b):
    return _bdot(a.astype(BF16), b.astype(BF16))


def _mm_nt(a, b):
    return lax.dot_general(a.astype(BF16), b.astype(BF16), (((1,), (1,)), ((), ())),
                           preferred_element_type=F32)


def _mm_tn(a, b):
    return lax.dot_general(a.astype(BF16), b.astype(BF16), (((0,), (0,)), ((), ())),
                           preferred_element_type=F32)


def _split3(a):
    hi = a.astype(BF16)
    r = a - hi.astype(F32)
    mid = r.astype(BF16)
    lo = (r - mid.astype(F32)).astype(BF16)
    return hi, mid, lo


def _mm_exact_lhs(a_bf16, b):
    out = None
    for part in _split3(b):
        term = _bdot(a_bf16, part)
        out = term if out is None else out + term
    return out


def _layer_norm(x, g, b):
    mu = jnp.mean(x, axis=-1, keepdims=True)
    xc = x - mu
    var = jnp.mean(xc * xc, axis=-1, keepdims=True)
    return xc * lax.rsqrt(var + LN_EPS) * g + b


def _sigmoid(x):
    return 1.0 / (1.0 + jnp.exp(-x))


def _silu(x):
    return x * _sigmoid(x)


def _gelu_tanh(x):
    return 0.5 * x * (1.0 + jnp.tanh(math.sqrt(2.0 / math.pi) * (x + 0.044715 * (x * x * x))))


def _softplus(x):
    return jnp.maximum(x, 0.0) + jnp.log1p(jnp.exp(-jnp.abs(x)))


def _run(jobs, count):
    for _ in range(min(count, len(jobs))):
        jobs.pop(0)()


def _unit_lower_inverses(lms, row, col, fillers):
    n = lms[0].shape[0]
    dist = row ^ col
    eye = (row == col).astype(F32)
    lds = [jnp.where(dist < SUBLANES, lm, 0.0) for lm in lms]
    ldb = [ld.astype(BF16) for ld in lds]
    xs = [eye - ld for ld in lds]
    pb = [_bdot(b, b).astype(BF16) for b in ldb]
    _run(fillers, 1)
    xs = [x + _bdot(x.astype(BF16), p) for x, p in zip(xs, pb)]
    pb = [_bdot(p, p).astype(BF16) for p in pb]
    _run(fillers, 1)
    xs = [x + _bdot(x.astype(BF16), p) for x, p in zip(xs, pb)]
    _run(fillers, 1)
    size = SUBLANES
    while size < n:
        band = jnp.logical_and(dist >= size, dist < 2 * size)
        eb = [jnp.where(band, lm, 0.0).astype(BF16) for lm in lms]
        xb = [x.astype(BF16) for x in xs]
        ys = [_bdot(e, x).astype(BF16) for e, x in zip(eb, xb)]
        xs = [x - _bdot(b, y) for x, b, y in zip(xs, xb, ys)]
        _run(fillers, 1)
        size *= 2
    return xs


def _mixer_kernel(apply_ln_in, x_ref, lng_ref, lnb_ref,
                  wqkv_ref, wz_ref, wba_ref, gcw_ref, alog_ref, dtb_ref, ng_ref,
                  wa_ref, wuv_ref, wg_ref, bg_ref, cw_ref, cb_ref, clg_ref, clb_ref, pa_ref, pb_ref,
                  slg_ref, slb_ref, ws_ref, bst_ref, pc_ref, wo_ref, l1g_ref, l1b_ref,
                  o_ref, cbuf_ref, s_ref, hbuf_ref):
    ts = x_ref.shape[1]
    t = pl.program_id(1)

    @pl.when(t == 0)
    def _():
        cbuf_ref[0:GDN_CONV_HIST, :] = jnp.zeros((GDN_CONV_HIST, cbuf_ref.shape[1]), F32)
        s_ref[...] = jnp.zeros(s_ref.shape, F32)
        hbuf_ref[0:CONV_HIST, :] = jnp.zeros((CONV_HIST, CONV_DIM), F32)

    x = x_ref[0]
    if apply_ln_in:
        x = _layer_norm(x, lng_ref[...], lnb_ref[...])
    xb = x.astype(BF16)

    n_chunk = D_MODEL // MIX_COLS
    guv_cols, gate_cols = [], []

    def uv_job(i):
        guv_cols.append(_gelu_tanh(_bdot(xb, wuv_ref[:, i * MIX_COLS:(i + 1) * MIX_COLS])))

    def gate_job(i):
        cols = slice(i * MIX_COLS, (i + 1) * MIX_COLS)
        gate_cols.append(_sigmoid(_bdot(xb, wg_ref[:, cols]) + bg_ref[:, cols]))

    x_jobs = [functools.partial(uv_job, i) for i in range(n_chunk)]
    x_jobs += [functools.partial(gate_job, i) for i in range(3 * n_chunk)]

    pqkv = _bdot(xb, wqkv_ref[...])
    ba = _bdot(xb, wba_ref[...])
    z = _bdot(xb, wz_ref[...])
    pa = _bdot(xb, wa_ref[...])
    hbuf_ref[CONV_HIST:CONV_HIST + ts, :] = pa[:, :CONV_DIM] * _sigmoid(pa[:, CONV_DIM:])

    cbuf_ref[GDN_CONV_HIST:GDN_CONV_HIST + ts, :] = pqkv
    sconv = gcw_ref[GDN_CONV - 1:GDN_CONV, :] * pqkv
    for j in range(GDN_CONV - 1):
        off = GDN_CONV_HIST - (GDN_CONV - 1) + j
        sconv = sconv + gcw_ref[j:j + 1, :] * cbuf_ref[off:off + ts, :]
    cbuf_ref[0:GDN_CONV_HIST, :] = cbuf_ref[ts:ts + GDN_CONV_HIST, :]
    _run(x_jobs, 3)
    qkv = _silu(sconv)
    _run(x_jobs, 3)

    beta_all = _sigmoid(ba)
    g_all = -jnp.exp(alog_ref[...]) * _softplus(ba + dtb_ref[...])
    row = lax.broadcasted_iota(jnp.int32, (ts, ts), 0)
    col = lax.broadcasted_iota(jnp.int32, (ts, ts), 1)
    ge = row >= col
    gt = row > col
    gam_all = _mm_exact_lhs(ge.astype(BF16), g_all)
    gam_all_t = gam_all.T

    heads = range(GDN_HEADS)
    qs, ks, vs, betas, gams, decays, lms = [], [], [], [], [], [], []
    for h in heads:
        q = qkv[:, h * GDN_DK:(h + 1) * GDN_DK]
        k = qkv[:, GDN_QK + h * GDN_DK:GDN_QK + (h + 1) * GDN_DK]
        qs.append(q * lax.rsqrt(jnp.sum(q * q, axis=-1, keepdims=True) + RMS_EPS) * (GDN_DK ** -0.5))
        ks.append(k * lax.rsqrt(jnp.sum(k * k, axis=-1, keepdims=True) + RMS_EPS))
        vs.append(qkv[:, 2 * GDN_QK + h * GDN_DV:2 * GDN_QK + (h + 1) * GDN_DV])
        betas.append(jnp.broadcast_to(beta_all[:, h:h + 1], (ts, LANES)))
        gams.append(jnp.broadcast_to(gam_all[:, GDN_HEADS + h:GDN_HEADS + h + 1], (ts, LANES)))
        _run(x_jobs, 1)
    for h in heads:
        gam_i = jnp.concatenate([gams[h]] * (ts // LANES), axis=1)
        gam_j = jnp.broadcast_to(gam_all_t[GDN_HEADS + h:GDN_HEADS + h + 1, :], (ts, ts))
        decays.append(jnp.where(ge, jnp.exp(jnp.minimum(gam_i - gam_j, 0.0)), 0.0))
        beta_i = jnp.concatenate([betas[h]] * (ts // LANES), axis=1)
        lms.append(jnp.where(gt, beta_i * _mm_nt(ks[h], ks[h]) * decays[h], 0.0))
        _run(x_jobs, 1)
    _run(x_jobs, len(x_jobs))

    first = CONV_HIST - (CONV_WIDTH - 1)
    conv_parts = []

    def conv_group(s):
        taps = [j for j in range(CONV_WIDTH) if (first + j) % SUBLANES == s]
        rows = ts if s == 0 else ts + SUBLANES
        part = None
        for j in taps:
            base = first + j - s
            term = cw_ref[j:j + 1, :] * hbuf_ref[base:base + rows, :]
            part = term if part is None else part + term
        if s:
            part = pltpu.roll(part, rows - s, axis=0)
        conv_parts.append(part[:ts, :])

    tinvs = _unit_lower_inverses(lms, row, col, [functools.partial(conv_group, s) for s in range(SUBLANES)])
    hbuf_ref[0:CONV_HIST, :] = hbuf_ref[ts:ts + CONV_HIST, :]
    conv = jnp.broadcast_to(cb_ref[...], (ts, CONV_DIM))
    for part in conv_parts:
        conv = conv + part

    egams = [jnp.exp(g) for g in gams]
    uws = [_mm(tinvs[h], jnp.concatenate([betas[h] * vs[h], betas[h] * ks[h] * egams[h]], axis=1))
           for h in heads]
    guv = jnp.concatenate(guv_cols, axis=1)
    aqks = [jnp.where(ge, _mm_nt(qs[h], ks[h]) * decays[h], 0.0) for h in heads]
    glasts = [g[ts - 1:ts, :] for g in gams]
    ss = [s_ref[h] for h in heads]
    v_news = [uws[h][:, :GDN_DV] - _mm(uws[h][:, GDN_DV:], ss[h]) for h in heads]
    u_sgu = guv[:, :SGU_DIM]
    vb = _layer_norm(guv[:, SGU_DIM:], slg_ref[...], slb_ref[...]).astype(BF16)
    for h in heads:
        s_ref[h] = ss[h] * jnp.exp(glasts[h]) + _mm_tn(ks[h] * jnp.exp(glasts[h] - gams[h]), v_news[h])
    gates = [jnp.concatenate(gate_cols[i * n_chunk:(i + 1) * n_chunk], axis=1) for i in range(3)]
    ogs = []
    for h in heads:
        o = _mm(qs[h] * egams[h], ss[h]) + _mm(aqks[h], v_news[h])
        o = o * lax.rsqrt(jnp.mean(o * o, axis=-1, keepdims=True) + RMS_EPS) * ng_ref[...]
        ogs.append(o * _silu(z[:, h * GDN_DV:(h + 1) * GDN_DV]))
    yb = _mm(jnp.concatenate(ogs, axis=1), pb_ref[...])

    prow = lax.broadcasted_iota(jnp.int32, (SGU_CHUNK, SGU_CHUNK), 0)
    pcol = lax.broadcasted_iota(jnp.int32, (SGU_CHUNK, SGU_CHUNK), 1)
    w_causal = [jnp.where(prow >= pcol, ws_ref[g], 0.0).astype(BF16) for g in range(SGU_GROUPS)]
    chunks = []
    for ci in range(ts // SGU_CHUNK):
        r0 = ci * SGU_CHUNK
        groups = []
        for g in range(SGU_GROUPS):
            vg = vb[r0:r0 + SGU_CHUNK, g * SGU_GROUP_DIM:(g + 1) * SGU_GROUP_DIM]
            groups.append(_bdot(w_causal[g], vg) + bst_ref[:, g:g + 1])
        chunks.append(jnp.concatenate(groups, axis=1))
    mixed = jnp.concatenate(chunks, axis=0)
    yc = _mm(u_sgu * mixed, pc_ref[...])

    ya = _mm(_silu(_layer_norm(conv, clg_ref[...], clb_ref[...])), pa_ref[...])
    merged = gates[0] * ya + gates[1] * yb + gates[2] * yc
    m = _mm(merged, wo_ref[...])
    o_ref[0] = _layer_norm(DN_ALPHA * x + m, l1g_ref[...], l1b_ref[...])


def _ffn_kernel(x_ref, w1_ref, b1_ref, w2_ref, b2_ref, g_ref, b_ref, o_ref):
    x = x_ref[0]
    xb = x.astype(BF16)
    acc = None
    for ci in range(D_FF // FFN_CHUNK):
        lo, hi = ci * FFN_CHUNK, (ci + 1) * FFN_CHUNK
        h = _bdot(xb, w1_ref[:, lo:hi]) + b1_ref[:, lo:hi]
        h = jnp.square(jnp.maximum(h, 0.0))
        part = _bdot(h.astype(BF16), w2_ref[lo:hi, :])
        acc = part if acc is None else acc + part
    o_ref[0] = _layer_norm(DN_ALPHA * x + (acc + b2_ref[...]), g_ref[...], b_ref[...])


def _resident_spec(arr):
    zeros = (0,) * arr.ndim
    return pl.BlockSpec(arr.shape, lambda b, t: zeros, pipeline_mode=pl.Buffered(1))


def _tile_spec(tile, width):
    return pl.BlockSpec((1, tile, width), lambda b, t: (b, t, 0))


def _compiler_params():
    return pltpu.CompilerParams(dimension_semantics=("arbitrary", "arbitrary"),
                                vmem_limit_bytes=VMEM_LIMIT_BYTES)


def _row(v):
    return v.reshape(1, -1).astype(F32)


def _mixer_call(x, apply_ln_in, params):
    bsz, seq, _ = x.shape
    tile = min(MIX_TILE, seq)
    return pl.pallas_call(
        functools.partial(_mixer_kernel, apply_ln_in),
        grid=(bsz, seq // tile),
        in_specs=[_tile_spec(tile, D_MODEL)] + [_resident_spec(p) for p in params],
        out_specs=_tile_spec(tile, D_MODEL),
        out_shape=jax.ShapeDtypeStruct((bsz, seq, D_MODEL), F32),
        scratch_shapes=[pltpu.VMEM((tile + GDN_CONV_HIST, 2 * GDN_QK + GDN_V), F32),
                        pltpu.VMEM((GDN_HEADS, GDN_DK, GDN_DV), F32),
                        pltpu.VMEM((tile + CONV_HIST, CONV_DIM), F32)],
        compiler_params=_compiler_params(),
        name="mixer",
    )(x, *params)


def _ffn_call(x, params):
    bsz, seq, _ = x.shape
    tile = min(FFN_TILE, seq)
    return pl.pallas_call(
        _ffn_kernel,
        grid=(bsz, seq // tile),
        in_specs=[_tile_spec(tile, D_MODEL)] + [_resident_spec(p) for p in params],
        out_specs=_tile_spec(tile, D_MODEL),
        out_shape=jax.ShapeDtypeStruct((bsz, seq, D_MODEL), F32),
        compiler_params=_compiler_params(),
        name="ffn",
    )(x, *params)


def kernel(x, ln_in_g, ln_in_b, w_in, b_gate, conv_dw_w, conv_dw_b, conv_ln_g, conv_ln_b, w_pa, gdn_conv_q, gdn_conv_k, gdn_conv_v, gdn_a_log, gdn_dt_bias, gdn_norm_g, w_pb, sgu_ln_g, sgu_ln_b, sgu_w_s, sgu_b_s, w_pc, w_o, ln1_g, ln1_b, w_ff1, b_ff1, w_ff2, b_ff2, ln2_g, ln2_b):
    o_a = 0
    o_q = o_a + 2 * CONV_DIM
    o_z = o_q + 2 * GDN_QK + GDN_V
    o_b = o_z + GDN_V
    o_uv = o_b + 2 * GDN_HEADS
    o_g = o_uv + 2 * SGU_DIM
    head_pad = LANES - 2 * GDN_HEADS
    lng, lnb = _row(ln_in_g), _row(ln_in_b)
    for l in range(DEPTH):
        w = w_in[l]
        wa = w[:, o_a:o_q].astype(BF16)
        wqkv = w[:, o_q:o_z].astype(BF16)
        wz = w[:, o_z:o_b].astype(BF16)
        wba = jnp.pad(w[:, o_b:o_uv], ((0, 0), (0, head_pad))).astype(BF16)
        wuv = w[:, o_uv:o_g].astype(BF16)
        wg = w[:, o_g:].astype(BF16)
        cw_qkv = jnp.concatenate([gdn_conv_q[l], gdn_conv_k[l], gdn_conv_v[l]], axis=1)
        alog = jnp.pad(gdn_a_log[l], (GDN_HEADS, head_pad)).reshape(1, LANES)
        dtb = jnp.pad(gdn_dt_bias[l], (GDN_HEADS, head_pad)).reshape(1, LANES)
        mixer_params = (lng, lnb,
                        wqkv, wz, wba, cw_qkv, alog, dtb, _row(gdn_norm_g[l]),
                        wa, wuv, wg, _row(b_gate[l]), conv_dw_w[l], _row(conv_dw_b[l]),
                        _row(conv_ln_g[l]), _row(conv_ln_b[l]), w_pa[l].astype(BF16), w_pb[l].astype(BF16),
                        _row(sgu_ln_g[l]), _row(sgu_ln_b[l]), sgu_w_s[l], sgu_b_s[l].T, w_pc[l].astype(BF16),
                        w_o[l].astype(BF16), _row(ln1_g[l]), _row(ln1_b[l]))
        x = _mixer_call(x, l == 0, mixer_params)
        ffn_params = (w_ff1[l].astype(BF16), _row(b_ff1[l]), w_ff2[l].astype(BF16), _row(b_ff2[l]),
                      _row(ln2_g[l]), _row(ln2_b[l]))
        x = _ffn_call(x, ffn_params)
    return x
```

```python
import functools
import math

import jax
import jax.numpy as jnp
from jax import lax
from jax.experimental import pallas as pl
from jax.experimental.pallas import tpu as pltpu

F32 = jnp.float32
BF16 = jnp.bfloat16

D_MODEL = 1024
DEPTH = 2
CONV_DIM = 512
CONV_WIDTH = 31
GDN_HEADS = 4
GDN_DK = 128
GDN_DV = 128
GDN_QK = GDN_HEADS * GDN_DK
GDN_V = GDN_HEADS * GDN_DV
GDN_CONV = 4
SGU_GROUPS = 4
SGU_GROUP_DIM = 128
SGU_DIM = SGU_GROUPS * SGU_GROUP_DIM
SGU_CHUNK = 128
D_FF = 4 * D_MODEL
DN_ALPHA = (2 * DEPTH) ** 0.25
LN_EPS = 1e-5
RMS_EPS = 1e-6

SUBLANES = 8
LANES = 128
VMEM_LIMIT_BYTES = 56 * 1024 * 1024

MIX_TILE = 128
MIX_ROWS = 2
FFN_TILE = 512
FFN_CHUNK = 1024
MIX_COLS = 256
CONV_HIST = 32
INV_CAT = 64
GDN_CONV_HIST = 8


def _bdot(a, b):
    return jnp.dot(a, b, preferred_element_type=F32)


def _mm(a, b):
    return _bdot(a.astype(BF16), b.astype(BF16))


def _mm_nt(a, b):
    return lax.dot_general(a.astype(BF16), b.astype(BF16), (((1,), (1,)), ((), ())),
                           preferred_element_type=F32)


def _mm_tn(a, b):
    return lax.dot_general(a.astype(BF16), b.astype(BF16), (((0,), (0,)), ((), ())),
                           preferred_element_type=F32)


def _split3(a):
    hi = a.astype(BF16)
    r = a - hi.astype(F32)
    mid = r.astype(BF16)
    lo = (r - mid.astype(F32)).astype(BF16)
    return hi, mid, lo


def _mm_exact_lhs(a_bf16, b):
    out = None
    for part in _split3(b):
        term = _bdot(a_bf16, part)
        out = term if out is None else out + term
    return out


def _layer_norm(x, g, b):
    mu = jnp.mean(x, axis=-1, keepdims=True)
    xc = x - mu
    var = jnp.mean(xc * xc, axis=-1, keepdims=True)
    return xc * lax.rsqrt(var + LN_EPS) * g + b


def _sigmoid(x):
    return 1.0 / (1.0 + jnp.exp(-x))


def _silu(x):
    return x * _sigmoid(x)


def _gelu_tanh(x):
    return 0.5 * x * (1.0 + jnp.tanh(math.sqrt(2.0 / math.pi) * (x + 0.044715 * (x * x * x))))


def _softplus(x):
    return jnp.maximum(x, 0.0) + jnp.log1p(jnp.exp(-jnp.abs(x)))


def _run(jobs, count):
    for _ in range(min(count, len(jobs))):
        jobs.pop(0)()


def _unit_lower_inverses(lms, fillers):
    n = lms[0].shape[0]
    c = min(INV_CAT, n)
    g = n // c
    crow = lax.broadcasted_iota(jnp.int32, (c, n), 0)
    ccol = lax.broadcasted_iota(jnp.int32, (c, n), 1)
    cdist = crow ^ (ccol & (c - 1))
    frow = lax.broadcasted_iota(jnp.int32, (n, n), 0)
    fcol = lax.broadcasted_iota(jnp.int32, (n, n), 1)
    fdist = frow ^ fcol
    same_block = fdist < c
    block_mask = same_block.astype(BF16)

    def fold(m):
        out = m[(g - 1) * c:]
        for b in range(g - 2, -1, -1):
            out = jnp.where(ccol < (b + 1) * c, m[b * c:(b + 1) * c], out)
        return out

    def cat_dot(a, b):
        return _bdot(a, jnp.concatenate([b] * g, axis=0) * block_mask)

    lcs = [fold(lm) for lm in lms]
    lds = [jnp.where(cdist < SUBLANES, lc, 0.0) for lc in lcs]
    ldb = [ld.astype(BF16) for ld in lds]
    eye = (cdist == 0).astype(F32)
    xs = [eye - ld for ld in lds]
    pb = [cat_dot(b, b).astype(BF16) for b in ldb]
    xs = [x + cat_dot(x.astype(BF16), p) for x, p in zip(xs, pb)]
    pb = [cat_dot(p, p).astype(BF16) for p in pb]
    xs = [x + cat_dot(x.astype(BF16), p) for x, p in zip(xs, pb)]
    size = SUBLANES
    while size < c:
        band = jnp.logical_and(cdist >= size, cdist < 2 * size)
        eb = [jnp.where(band, lc, 0.0).astype(BF16) for lc in lcs]
        xb = [x.astype(BF16) for x in xs]
        ys = [cat_dot(e, x).astype(BF16) for e, x in zip(eb, xb)]
        _run(fillers, 1)
        xs = [x - cat_dot(b, y) for x, b, y in zip(xs, xb, ys)]
        size *= 2
    xs = [jnp.where(same_block, jnp.concatenate([x] * g, axis=0), 0.0) for x in xs]
    while size < n:
        band = jnp.logical_and(fdist >= size, fdist < 2 * size)
        eb = [jnp.where(band, lm, 0.0).astype(BF16) for lm in lms]
        xb = [x.astype(BF16) for x in xs]
        ys = [_bdot(e, x).astype(BF16) for e, x in zip(eb, xb)]
        _run(fillers, 1)
        xs = [x - _bdot(b, y) for x, b, y in zip(xs, xb, ys)]
        size *= 2
    return xs


def _mixer_kernel(apply_ln_in, x_ref, lng_ref, lnb_ref,
                  wqkv_ref, wz_ref, wba_ref, gcw_ref, alog_ref, dtb_ref, ng_ref,
                  wa_ref, wuv_ref, wg_ref, bg_ref, cw_ref, cb_ref, clg_ref, clb_ref, pa_ref, pb_ref,
                  slg_ref, slb_ref, ws_ref, bst_ref, pc_ref, wo_ref, l1g_ref, l1b_ref,
                  o_ref, cbuf_ref, s_ref, hbuf_ref):
    nr, ts = x_ref.shape[0], x_ref.shape[1]
    rows = [slice(r * ts, (r + 1) * ts) for r in range(nr)]
    t = pl.program_id(1)

    @pl.when(t == 0)
    def _():
        cbuf_ref[:, 0:GDN_CONV_HIST, :] = jnp.zeros((nr, GDN_CONV_HIST, cbuf_ref.shape[2]), F32)
        s_ref[...] = jnp.zeros(s_ref.shape, F32)
        hbuf_ref[:, 0:CONV_HIST, :] = jnp.zeros((nr, CONV_HIST, CONV_DIM), F32)

    x = x_ref[...].reshape(nr * ts, D_MODEL)
    if apply_ln_in:
        x = _layer_norm(x, lng_ref[...], lnb_ref[...])
    xb = x.astype(BF16)

    n_chunk = D_MODEL // MIX_COLS
    guv_cols, gate_cols = [], []

    def uv_job(i):
        guv_cols.append(_gelu_tanh(_bdot(xb, wuv_ref[:, i * MIX_COLS:(i + 1) * MIX_COLS])))

    def gate_job(i):
        cols = slice(i * MIX_COLS, (i + 1) * MIX_COLS)
        gate_cols.append(_sigmoid(_bdot(xb, wg_ref[:, cols]) + bg_ref[:, cols]))

    x_jobs = [functools.partial(uv_job, i) for i in range(n_chunk)]
    x_jobs += [functools.partial(gate_job, i) for i in range(3 * n_chunk)]

    pqkv = _bdot(xb, wqkv_ref[...])
    ba = _bdot(xb, wba_ref[...])
    z = _bdot(xb, wz_ref[...])
    pa = _bdot(xb, wa_ref[...])
    glu = pa[:, :CONV_DIM] * _sigmoid(pa[:, CONV_DIM:])
    for r in range(nr):
        hbuf_ref[r, CONV_HIST:CONV_HIST + ts, :] = glu[rows[r]]

    first = CONV_HIST - (CONV_WIDTH - 1)
    conv_strips = {}

    def conv_strip(r, i):
        lanes = slice(i * LANES, (i + 1) * LANES)
        acc = jnp.broadcast_to(cb_ref[:, lanes], (ts, LANES))
        for s in range(SUBLANES):
            taps = [j for j in range(CONV_WIDTH) if (first + j) % SUBLANES == s]
            nrow = ts if s == 0 else ts + SUBLANES
            part = None
            for j in taps:
                base = first + j - s
                term = cw_ref[j:j + 1, lanes] * hbuf_ref[r, base:base + nrow, lanes]
                part = term if part is None else part + term
            if s:
                part = pltpu.roll(part, nrow - s, axis=0)
            acc = acc + part[:ts, :]
        conv_strips[(r, i)] = acc

    v_jobs = [functools.partial(conv_strip, r, i) for i in range(CONV_DIM // LANES) for r in range(nr)]

    sconvs = []
    for r in range(nr):
        cbuf_ref[r, GDN_CONV_HIST:GDN_CONV_HIST + ts, :] = pqkv[rows[r]]
        sconv = gcw_ref[GDN_CONV - 1:GDN_CONV, :] * pqkv[rows[r]]
        for j in range(GDN_CONV - 1):
            off = GDN_CONV_HIST - (GDN_CONV - 1) + j
            sconv = sconv + gcw_ref[j:j + 1, :] * cbuf_ref[r, off:off + ts, :]
        cbuf_ref[r, 0:GDN_CONV_HIST, :] = cbuf_ref[r, ts:ts + GDN_CONV_HIST, :]
        sconvs.append(sconv)
        _run(v_jobs, 1)
    qkv = _silu(jnp.concatenate(sconvs, axis=0))
    _run(v_jobs, 1)

    beta_all = _sigmoid(ba)
    g_all = -jnp.exp(alog_ref[...]) * _softplus(ba + dtb_ref[...])
    row = lax.broadcasted_iota(jnp.int32, (ts, ts), 0)
    col = lax.broadcasted_iota(jnp.int32, (ts, ts), 1)
    ge = row >= col
    gt = row > col
    ltri = ge.astype(BF16)
    gam_alls = [_mm_exact_lhs(ltri, g_all[rows[r]]) for r in range(nr)]
    gam_all_ts = [g.T for g in gam_alls]

    heads = range(GDN_HEADS)
    chains = [(r, h) for r in range(nr) for h in heads]
    qn, kn = [], []
    for h in heads:
        q = qkv[:, h * GDN_DK:(h + 1) * GDN_DK]
        k = qkv[:, GDN_QK + h * GDN_DK:GDN_QK + (h + 1) * GDN_DK]
        qn.append(q * lax.rsqrt(jnp.sum(q * q, axis=-1, keepdims=True) + RMS_EPS) * (GDN_DK ** -0.5))
        kn.append(k * lax.rsqrt(jnp.sum(k * k, axis=-1, keepdims=True) + RMS_EPS))
        _run(v_jobs, 1)
    qs, ks, vs, betas, gams, decays, lms = {}, {}, {}, {}, {}, {}, {}
    for r, h in chains:
        qs[r, h] = qn[h][rows[r]]
        ks[r, h] = kn[h][rows[r]]
        vs[r, h] = qkv[rows[r], 2 * GDN_QK + h * GDN_DV:2 * GDN_QK + (h + 1) * GDN_DV]
        betas[r, h] = jnp.broadcast_to(beta_all[rows[r], h:h + 1], (ts, LANES))
        gams[r, h] = jnp.broadcast_to(gam_alls[r][:, GDN_HEADS + h:GDN_HEADS + h + 1], (ts, LANES))
    for r, h in chains:
        gam_i = jnp.concatenate([gams[r, h]] * (ts // LANES), axis=1)
        gam_j = jnp.broadcast_to(gam_all_ts[r][GDN_HEADS + h:GDN_HEADS + h + 1, :], (ts, ts))
        decays[r, h] = jnp.where(ge, jnp.exp(jnp.minimum(gam_i - gam_j, 0.0)), 0.0)
        beta_i = jnp.concatenate([betas[r, h]] * (ts // LANES), axis=1)
        lms[r, h] = jnp.where(gt, beta_i * _mm_nt(ks[r, h], ks[r, h]) * decays[r, h], 0.0)
        _run(v_jobs, 1)
    _run(v_jobs, len(v_jobs))
    for r in range(nr):
        hbuf_ref[r, 0:CONV_HIST, :] = hbuf_ref[r, ts:ts + CONV_HIST, :]
    tinv_list = _unit_lower_inverses([lms[c] for c in chains], x_jobs)
    tinvs = dict(zip(chains, tinv_list))
    _run(x_jobs, len(x_jobs))

    egams = {c: jnp.exp(gams[c]) for c in chains}
    uws = {c: _mm(tinvs[c], jnp.concatenate([betas[c] * vs[c], betas[c] * ks[c] * egams[c]], axis=1))
           for c in chains}
    guv = jnp.concatenate(guv_cols, axis=1)
    aqks = {c: jnp.where(ge, _mm_nt(qs[c], ks[c]) * decays[c], 0.0) for c in chains}
    glasts = {c: gams[c][ts - 1:ts, :] for c in chains}
    ss = {(r, h): s_ref[r, h] for r, h in chains}
    v_news = {c: uws[c][:, :GDN_DV] - _mm(uws[c][:, GDN_DV:], ss[c]) for c in chains}
    u_sgu = guv[:, :SGU_DIM]
    vb = _layer_norm(guv[:, SGU_DIM:], slg_ref[...], slb_ref[...]).astype(BF16)
    for r, h in chains:
        c = (r, h)
        s_ref[r, h] = ss[c] * jnp.exp(glasts[c]) + _mm_tn(ks[c] * jnp.exp(glasts[c] - gams[c]), v_news[c])
    gates = [jnp.concatenate(gate_cols[i * n_chunk:(i + 1) * n_chunk], axis=1) for i in range(3)]
    ogs = {}
    for r, h in chains:
        c = (r, h)
        o = _mm(qs[c] * egams[c], ss[c]) + _mm(aqks[c], v_news[c])
        ogs[c] = o * lax.rsqrt(jnp.mean(o * o, axis=-1, keepdims=True) + RMS_EPS) * ng_ref[...]
    og = jnp.concatenate([jnp.concatenate([ogs[r, h] for h in heads], axis=1) for r in range(nr)], axis=0)
    yb = _mm(og * _silu(z), pb_ref[...])

    prow = lax.broadcasted_iota(jnp.int32, (SGU_CHUNK, SGU_CHUNK), 0)
    pcol = lax.broadcasted_iota(jnp.int32, (SGU_CHUNK, SGU_CHUNK), 1)
    w_causal = [jnp.where(prow >= pcol, ws_ref[g], 0.0).astype(BF16) for g in range(SGU_GROUPS)]
    chunks = []
    for ci in range(nr * ts // SGU_CHUNK):
        r0 = ci * SGU_CHUNK
        groups = []
        for g in range(SGU_GROUPS):
            vg = vb[r0:r0 + SGU_CHUNK, g * SGU_GROUP_DIM:(g + 1) * SGU_GROUP_DIM]
            groups.append(_bdot(w_causal[g], vg) + bst_ref[:, g:g + 1])
        chunks.append(jnp.concatenate(groups, axis=1))
    mixed = jnp.concatenate(chunks, axis=0)
    yc = _mm(u_sgu * mixed, pc_ref[...])

    conv = jnp.concatenate([jnp.concatenate([conv_strips[r, i] for i in range(CONV_DIM // LANES)], axis=1)
                            for r in range(nr)], axis=0)
    ya = _mm(_silu(_layer_norm(conv, clg_ref[...], clb_ref[...])), pa_ref[...])
    merged = gates[0] * ya + gates[1] * yb + gates[2] * yc
    m = _mm(merged, wo_ref[...])
    out = _layer_norm(DN_ALPHA * x + m, l1g_ref[...], l1b_ref[...])
    o_ref[...] = out.reshape(nr, ts, D_MODEL)


def _ffn_kernel(x_ref, w1_ref, b1_ref, w2_ref, b2_ref, g_ref, b_ref, o_ref):
    x = x_ref[0]
    xb = x.astype(BF16)
    acc = None
    for ci in range(D_FF // FFN_CHUNK):
        lo, hi = ci * FFN_CHUNK, (ci + 1) * FFN_CHUNK
        h = _bdot(xb, w1_ref[:, lo:hi]) + b1_ref[:, lo:hi]
        h = jnp.square(jnp.maximum(h, 0.0))
        part = _bdot(h.astype(BF16), w2_ref[lo:hi, :])
        acc = part if acc is None else acc + part
    o_ref[0] = _layer_norm(DN_ALPHA * x + (acc + b2_ref[...]), g_ref[...], b_ref[...])


def _resident_spec(arr):
    zeros = (0,) * arr.ndim
    return pl.BlockSpec(arr.shape, lambda b, t: zeros, pipeline_mode=pl.Buffered(1))


def _tile_spec(nrows, tile, width):
    return pl.BlockSpec((nrows, tile, width), lambda b, t: (b, t, 0))


def _compiler_params():
    return pltpu.CompilerParams(dimension_semantics=("arbitrary", "arbitrary"),
                                vmem_limit_bytes=VMEM_LIMIT_BYTES)


def _row(v):
    return v.reshape(1, -1).astype(F32)


def _mixer_call(x, apply_ln_in, params):
    bsz, seq, _ = x.shape
    tile = min(MIX_TILE, seq)
    nrows = MIX_ROWS if bsz % MIX_ROWS == 0 else 1
    return pl.pallas_call(
        functools.partial(_mixer_kernel, apply_ln_in),
        grid=(bsz // nrows, seq // tile),
        in_specs=[_tile_spec(nrows, tile, D_MODEL)] + [_resident_spec(p) for p in params],
        out_specs=_tile_spec(nrows, tile, D_MODEL),
        out_shape=jax.ShapeDtypeStruct((bsz, seq, D_MODEL), F32),
        scratch_shapes=[pltpu.VMEM((nrows, tile + GDN_CONV_HIST, 2 * GDN_QK + GDN_V), F32),
                        pltpu.VMEM((nrows, GDN_HEADS, GDN_DK, GDN_DV), F32),
                        pltpu.VMEM((nrows, tile + CONV_HIST, CONV_DIM), F32)],
        compiler_params=_compiler_params(),
        name="mixer",
    )(x, *params)


def _ffn_call(x, params):
    bsz, seq, _ = x.shape
    tile = min(FFN_TILE, seq)
    return pl.pallas_call(
        _ffn_kernel,
        grid=(bsz, seq // tile),
        in_specs=[_tile_spec(1, tile, D_MODEL)] + [_resident_spec(p) for p in params],
        out_specs=_tile_spec(1, tile, D_MODEL),
        out_shape=jax.ShapeDtypeStruct((bsz, seq, D_MODEL), F32),
        compiler_params=_compiler_params(),
        name="ffn",
    )(x, *params)


def kernel(x, ln_in_g, ln_in_b, w_in, b_gate, conv_dw_w, conv_dw_b, conv_ln_g, conv_ln_b, w_pa, gdn_conv_q, gdn_conv_k, gdn_conv_v, gdn_a_log, gdn_dt_bias, gdn_norm_g, w_pb, sgu_ln_g, sgu_ln_b, sgu_w_s, sgu_b_s, w_pc, w_o, ln1_g, ln1_b, w_ff1, b_ff1, w_ff2, b_ff2, ln2_g, ln2_b):
    o_a = 0
    o_q = o_a + 2 * CONV_DIM
    o_z = o_q + 2 * GDN_QK + GDN_V
    o_b = o_z + GDN_V
    o_uv = o_b + 2 * GDN_HEADS
    o_g = o_uv + 2 * SGU_DIM
    head_pad = LANES - 2 * GDN_HEADS
    lng, lnb = _row(ln_in_g), _row(ln_in_b)
    for l in range(DEPTH):
        w = w_in[l]
        wa = w[:, o_a:o_q].astype(BF16)
        wqkv = w[:, o_q:o_z].astype(BF16)
        wz = w[:, o_z:o_b].astype(BF16)
        wba = jnp.pad(w[:, o_b:o_uv], ((0, 0), (0, head_pad))).astype(BF16)
        wuv = w[:, o_uv:o_g].astype(BF16)
        wg = w[:, o_g:].astype(BF16)
        cw_qkv = jnp.concatenate([gdn_conv_q[l], gdn_conv_k[l], gdn_conv_v[l]], axis=1)
        alog = jnp.pad(gdn_a_log[l], (GDN_HEADS, head_pad)).reshape(1, LANES)
        dtb = jnp.pad(gdn_dt_bias[l], (GDN_HEADS, head_pad)).reshape(1, LANES)
        mixer_params = (lng, lnb,
                        wqkv, wz, wba, cw_qkv, alog, dtb, _row(gdn_norm_g[l]),
                        wa, wuv, wg, _row(b_gate[l]), conv_dw_w[l], _row(conv_dw_b[l]),
                        _row(conv_ln_g[l]), _row(conv_ln_b[l]), w_pa[l].astype(BF16), w_pb[l].astype(BF16),
                        _row(sgu_ln_g[l]), _row(sgu_ln_b[l]), sgu_w_s[l], sgu_b_s[l].T, w_pc[l].astype(BF16),
                        w_o[l].astype(BF16), _row(ln1_g[l]), _row(ln1_b[l]))
        x = _mixer_call(x, l == 0, mixer_params)
        ffn_params = (w_ff1[l].astype(BF16), _row(b_ff1[l]), w_ff2[l].astype(BF16), _row(b_ff2[l]),
                      _row(ln2_g[l]), _row(ln2_b[l]))
        x = _ffn_call(x, ffn_params)
    return x
```

```python
import functools
import math

import jax
import jax.numpy as jnp
from jax import lax
from jax.experimental import pallas as pl
from jax.experimental.pallas import tpu as pltpu

F32 = jnp.float32
BF16 = jnp.bfloat16

D_MODEL = 1024
DEPTH = 2
CONV_DIM = 512
CONV_WIDTH = 31
GDN_HEADS = 4
GDN_DK = 128
GDN_DV = 128
GDN_QK = GDN_HEADS * GDN_DK
GDN_V = GDN_HEADS * GDN_DV
GDN_CONV = 4
SGU_GROUPS = 4
SGU_GROUP_DIM = 128
SGU_DIM = SGU_GROUPS * SGU_GROUP_DIM
SGU_CHUNK = 128
D_FF = 4 * D_MODEL
DN_ALPHA = (2 * DEPTH) ** 0.25
LN_EPS = 1e-5
RMS_EPS = 1e-6

SUBLANES = 8
LANES = 128
VMEM_LIMIT_BYTES = 60 * 1024 * 1024

MIX_TILE = 128
MIX_ROWS = 2
FFN_CHUNK = 1024
MIX_COLS = 256
CONV_HIST = 32
INV_CAT = 64
GDN_CONV_HIST = 8


def _bdot(a, b):
    return jnp.dot(a, b, preferred_element_type=F32)


def _mm(a, b):
    return _bdot(a.astype(BF16), b.astype(BF16))


def _mm_nt(a, b):
    return lax.dot_general(a.astype(BF16), b.astype(BF16), (((1,), (1,)), ((), ())),
                           preferred_element_type=F32)


def _mm_tn(a, b):
    return lax.dot_general(a.astype(BF16), b.astype(BF16), (((0,), (0,)), ((), ())),
                           preferred_element_type=F32)


def _split3(a):
    hi = a.astype(BF16)
    r = a - hi.astype(F32)
    mid = r.astype(BF16)
    lo = (r - mid.astype(F32)).astype(BF16)
    return hi, mid, lo


def _mm_exact_lhs(a_bf16, b):
    out = None
    for part in _split3(b):
        term = _bdot(a_bf16, part)
        out = term if out is None else out + term
    return out


def _layer_norm(x, g, b):
    mu = jnp.mean(x, axis=-1, keepdims=True)
    xc = x - mu
    var = jnp.mean(xc * xc, axis=-1, keepdims=True)
    return xc * lax.rsqrt(var + LN_EPS) * g + b


def _sigmoid(x):
    return 1.0 / (1.0 + jnp.exp2(x * (-1.0 / math.log(2.0))))


def _silu(x):
    return x * _sigmoid(x)


def _gelu_tanh(x):
    return 0.5 * x * (1.0 + jnp.tanh(math.sqrt(2.0 / math.pi) * (x + 0.044715 * (x * x * x))))


def _softplus(x):
    return jnp.maximum(x, 0.0) + jnp.log1p(jnp.exp(-jnp.abs(x)))


def _run(jobs, count):
    for _ in range(min(count, len(jobs))):
        jobs.pop(0)()


def _unit_lower_inverses(lms, fillers):
    n = lms[0].shape[0]
    c = min(INV_CAT, n)
    g = n // c
    crow = lax.broadcasted_iota(jnp.int32, (c, n), 0)
    ccol = lax.broadcasted_iota(jnp.int32, (c, n), 1)
    cdist = crow ^ (ccol & (c - 1))
    frow = lax.broadcasted_iota(jnp.int32, (n, n), 0)
    fcol = lax.broadcasted_iota(jnp.int32, (n, n), 1)
    fdist = frow ^ fcol
    same_block = fdist < c
    block_mask = same_block.astype(BF16)

    def fold(m):
        out = m[(g - 1) * c:]
        for b in range(g - 2, -1, -1):
            out = jnp.where(ccol < (b + 1) * c, m[b * c:(b + 1) * c], out)
        return out

    def cat_dot(a, b):
        return _bdot(a, jnp.concatenate([b] * g, axis=0) * block_mask)

    lcs = [fold(lm) for lm in lms]
    lds = [jnp.where(cdist < SUBLANES, lc, 0.0) for lc in lcs]
    ldb = [ld.astype(BF16) for ld in lds]
    eye = (cdist == 0).astype(F32)
    xs = [eye - ld for ld in lds]
    pb = [cat_dot(b, b).astype(BF16) for b in ldb]
    xs = [x + cat_dot(x.astype(BF16), p) for x, p in zip(xs, pb)]
    pb = [cat_dot(p, p).astype(BF16) for p in pb]
    xs = [x + cat_dot(x.astype(BF16), p) for x, p in zip(xs, pb)]
    size = SUBLANES
    while size < c:
        band = jnp.logical_and(cdist >= size, cdist < 2 * size)
        eb = [jnp.where(band, lc, 0.0).astype(BF16) for lc in lcs]
        xb = [x.astype(BF16) for x in xs]
        ys = [cat_dot(e, x).astype(BF16) for e, x in zip(eb, xb)]
        _run(fillers, 1)
        xs = [x - cat_dot(b, y) for x, b, y in zip(xs, xb, ys)]
        size *= 2
    xs = [jnp.where(same_block, jnp.concatenate([x] * g, axis=0), 0.0) for x in xs]
    while size < n:
        band = jnp.logical_and(fdist >= size, fdist < 2 * size)
        eb = [jnp.where(band, lm, 0.0).astype(BF16) for lm in lms]
        xb = [x.astype(BF16) for x in xs]
        ys = [_bdot(e, x).astype(BF16) for e, x in zip(eb, xb)]
        _run(fillers, 1)
        xs = [x - _bdot(b, y) for x, b, y in zip(xs, xb, ys)]
        size *= 2
    return xs


def _layer_kernel(apply_ln_in, x_ref, lng_ref, lnb_ref,
                  wqkv_ref, wz_ref, wba_ref, gcw_ref, alog_ref, dtb_ref, ng_ref,
                  wa_ref, wuv_ref, wg_ref, bg_ref, cw_ref, cb_ref, clg_ref, clb_ref, pa_ref, pb_ref,
                  slg_ref, slb_ref, ws_ref, bst_ref, pc_ref, wo_ref, l1g_ref, l1b_ref,
                  w1_ref, b1_ref, w2_ref, b2_ref, l2g_ref, l2b_ref,
                  o_ref, cbuf_ref, s_ref, hbuf_ref, x1_ref):
    nr, ts = x_ref.shape[0], x_ref.shape[1]
    rows = [slice(r * ts, (r + 1) * ts) for r in range(nr)]
    t = pl.program_id(1)

    @pl.when(t == 0)
    def _():
        cbuf_ref[:, 0:GDN_CONV_HIST, :] = jnp.zeros((nr, GDN_CONV_HIST, cbuf_ref.shape[2]), F32)
        s_ref[...] = jnp.zeros(s_ref.shape, F32)
        hbuf_ref[:, 0:CONV_HIST, :] = jnp.zeros((nr, CONV_HIST, CONV_DIM), F32)
        x1_ref[...] = jnp.zeros(x1_ref.shape, F32)

    f_in = x1_ref[...]
    f_inb = f_in.astype(BF16)
    ffn_parts = []

    def ffn_job(ci):
        lo, hi = ci * FFN_CHUNK, (ci + 1) * FFN_CHUNK
        hid = jnp.square(jnp.maximum(_bdot(f_inb, w1_ref[:, lo:hi]) + b1_ref[:, lo:hi], 0.0))
        ffn_parts.append(_bdot(hid.astype(BF16), w2_ref[lo:hi, :]))

    f_jobs = [functools.partial(ffn_job, ci) for ci in range(D_FF // FFN_CHUNK)]

    x = x_ref[...].reshape(nr * ts, D_MODEL)
    if apply_ln_in:
        x = _layer_norm(x, lng_ref[...], lnb_ref[...])
    xb = x.astype(BF16)

    n_chunk = D_MODEL // MIX_COLS
    guv_cols, gate_cols = [], []

    def uv_job(i):
        guv_cols.append(_gelu_tanh(_bdot(xb, wuv_ref[:, i * MIX_COLS:(i + 1) * MIX_COLS])))

    def gate_job(i):
        cols = slice(i * MIX_COLS, (i + 1) * MIX_COLS)
        gate_cols.append(_sigmoid(_bdot(xb, wg_ref[:, cols]) + bg_ref[:, cols]))

    x_jobs = [functools.partial(uv_job, i) for i in range(n_chunk)]
    x_jobs += [functools.partial(gate_job, i) for i in range(3 * n_chunk)]

    pqkv = _bdot(xb, wqkv_ref[...])
    ba = _bdot(xb, wba_ref[...])
    z = _bdot(xb, wz_ref[...])
    pa = _bdot(xb, wa_ref[...])
    glu = pa[:, :CONV_DIM] * _sigmoid(pa[:, CONV_DIM:])
    for r in range(nr):
        hbuf_ref[r, CONV_HIST:CONV_HIST + ts, :] = glu[rows[r]]

    first = CONV_HIST - (CONV_WIDTH - 1)
    conv_strips = {}

    def conv_strip(r, i):
        lanes = slice(i * LANES, (i + 1) * LANES)
        acc = jnp.broadcast_to(cb_ref[:, lanes], (ts, LANES))
        for s in range(SUBLANES):
            taps = [j for j in range(CONV_WIDTH) if (first + j) % SUBLANES == s]
            nrow = ts if s == 0 else ts + SUBLANES
            part = None
            for j in taps:
                base = first + j - s
                term = cw_ref[j:j + 1, lanes] * hbuf_ref[r, base:base + nrow, lanes]
                part = term if part is None else part + term
            if s:
                part = pltpu.roll(part, nrow - s, axis=0)
            acc = acc + part[:ts, :]
        conv_strips[(r, i)] = acc

    v_jobs = [functools.partial(conv_strip, r, i) for i in range(CONV_DIM // LANES) for r in range(nr)]

    sconvs = []
    for r in range(nr):
        cbuf_ref[r, GDN_CONV_HIST:GDN_CONV_HIST + ts, :] = pqkv[rows[r]]
        sconv = gcw_ref[GDN_CONV - 1:GDN_CONV, :] * pqkv[rows[r]]
        for j in range(GDN_CONV - 1):
            off = GDN_CONV_HIST - (GDN_CONV - 1) + j
            sconv = sconv + gcw_ref[j:j + 1, :] * cbuf_ref[r, off:off + ts, :]
        cbuf_ref[r, 0:GDN_CONV_HIST, :] = cbuf_ref[r, ts:ts + GDN_CONV_HIST, :]
        sconvs.append(sconv)
        _run(v_jobs, 1)
    qkv = _silu(jnp.concatenate(sconvs, axis=0))
    _run(v_jobs, 1)

    beta_all = _sigmoid(ba)
    g_all = -jnp.exp(alog_ref[...]) * _softplus(ba + dtb_ref[...])
    row = lax.broadcasted_iota(jnp.int32, (ts, ts), 0)
    col = lax.broadcasted_iota(jnp.int32, (ts, ts), 1)
    ge = row >= col
    gt = row > col
    ltri = ge.astype(BF16)
    gam_alls = [_mm_exact_lhs(ltri, g_all[rows[r]]) for r in range(nr)]
    gam_all_ts = [g.T for g in gam_alls]

    heads = range(GDN_HEADS)
    chains = [(r, h) for r in range(nr) for h in heads]
    qn, kn = [], []
    for h in heads:
        q = qkv[:, h * GDN_DK:(h + 1) * GDN_DK]
        k = qkv[:, GDN_QK + h * GDN_DK:GDN_QK + (h + 1) * GDN_DK]
        qn.append(q * lax.rsqrt(jnp.sum(q * q, axis=-1, keepdims=True) + RMS_EPS) * (GDN_DK ** -0.5))
        kn.append(k * lax.rsqrt(jnp.sum(k * k, axis=-1, keepdims=True) + RMS_EPS))
        _run(v_jobs, 1)
    qs, ks, vs, betas, gams, decays, lms = {}, {}, {}, {}, {}, {}, {}
    for r, h in chains:
        qs[r, h] = qn[h][rows[r]]
        ks[r, h] = kn[h][rows[r]]
        vs[r, h] = qkv[rows[r], 2 * GDN_QK + h * GDN_DV:2 * GDN_QK + (h + 1) * GDN_DV]
        betas[r, h] = jnp.broadcast_to(beta_all[rows[r], h:h + 1], (ts, LANES))
        gams[r, h] = jnp.broadcast_to(gam_alls[r][:, GDN_HEADS + h:GDN_HEADS + h + 1], (ts, LANES))
    for r, h in chains:
        gam_i = jnp.concatenate([gams[r, h]] * (ts // LANES), axis=1)
        gam_j = jnp.broadcast_to(gam_all_ts[r][GDN_HEADS + h:GDN_HEADS + h + 1, :], (ts, ts))
        decays[r, h] = jnp.where(ge, jnp.exp(jnp.minimum(gam_i - gam_j, 0.0)), 0.0)
        beta_i = jnp.concatenate([betas[r, h]] * (ts // LANES), axis=1)
        lms[r, h] = jnp.where(gt, beta_i * _mm_nt(ks[r, h], ks[r, h]) * decays[r, h], 0.0)
        _run(v_jobs, 1)
    _run(v_jobs, len(v_jobs))
    for r in range(nr):
        hbuf_ref[r, 0:CONV_HIST, :] = hbuf_ref[r, ts:ts + CONV_HIST, :]
    tinv_list = _unit_lower_inverses([lms[c] for c in chains], x_jobs)
    tinvs = dict(zip(chains, tinv_list))
    _run(x_jobs, len(x_jobs))
    _run(f_jobs, 1)

    egams = {c: jnp.exp(gams[c]) for c in chains}
    uws = {c: _mm(tinvs[c], jnp.concatenate([betas[c] * vs[c], betas[c] * ks[c] * egams[c]], axis=1))
           for c in chains}
    _run(f_jobs, 1)
    guv = jnp.concatenate(guv_cols, axis=1)
    aqks = {c: jnp.where(ge, _mm_nt(qs[c], ks[c]) * decays[c], 0.0) for c in chains}
    glasts = {c: gams[c][ts - 1:ts, :] for c in chains}
    ss = {(r, h): s_ref[r, h] for r, h in chains}
    v_news = {c: uws[c][:, :GDN_DV] - _mm(uws[c][:, GDN_DV:], ss[c]) for c in chains}
    _run(f_jobs, 1)
    u_sgu = guv[:, :SGU_DIM]
    vb = _layer_norm(guv[:, SGU_DIM:], slg_ref[...], slb_ref[...]).astype(BF16)
    for r, h in chains:
        c = (r, h)
        s_ref[r, h] = ss[c] * jnp.exp(glasts[c]) + _mm_tn(ks[c] * jnp.exp(glasts[c] - gams[c]), v_news[c])
    gates = [jnp.concatenate(gate_cols[i * n_chunk:(i + 1) * n_chunk], axis=1) for i in range(3)]
    ogs = {}
    for r, h in chains:
        c = (r, h)
        o = _mm(qs[c] * egams[c], ss[c]) + _mm(aqks[c], v_news[c])
        ogs[c] = o * lax.rsqrt(jnp.mean(o * o, axis=-1, keepdims=True) + RMS_EPS) * ng_ref[...]
    _run(f_jobs, len(f_jobs))
    ffn = ffn_parts[0]
    for part in ffn_parts[1:]:
        ffn = ffn + part
    o_ref[...] = _layer_norm(DN_ALPHA * f_in + (ffn + b2_ref[...]), l2g_ref[...], l2b_ref[...]
                             ).reshape(nr, ts, D_MODEL)
    og = jnp.concatenate([jnp.concatenate([ogs[r, h] for h in heads], axis=1) for r in range(nr)], axis=0)
    yb = _mm(og * _silu(z), pb_ref[...])

    prow = lax.broadcasted_iota(jnp.int32, (SGU_CHUNK, SGU_CHUNK), 0)
    pcol = lax.broadcasted_iota(jnp.int32, (SGU_CHUNK, SGU_CHUNK), 1)
    w_causal = [jnp.where(prow >= pcol, ws_ref[g], 0.0).astype(BF16) for g in range(SGU_GROUPS)]
    chunks = []
    for ci in range(nr * ts // SGU_CHUNK):
        r0 = ci * SGU_CHUNK
        groups = []
        for g in range(SGU_GROUPS):
            vg = vb[r0:r0 + SGU_CHUNK, g * SGU_GROUP_DIM:(g + 1) * SGU_GROUP_DIM]
            groups.append(_bdot(w_causal[g], vg) + bst_ref[:, g:g + 1])
        chunks.append(jnp.concatenate(groups, axis=1))
    mixed = jnp.concatenate(chunks, axis=0)
    yc = _mm(u_sgu * mixed, pc_ref[...])

    conv = jnp.concatenate([jnp.concatenate([conv_strips[r, i] for i in range(CONV_DIM // LANES)], axis=1)
                            for r in range(nr)], axis=0)
    ya = _mm(_silu(_layer_norm(conv, clg_ref[...], clb_ref[...])), pa_ref[...])
    merged = gates[0] * ya + gates[1] * yb + gates[2] * yc
    m = _mm(merged, wo_ref[...])
    x1_ref[...] = _layer_norm(DN_ALPHA * x + m, l1g_ref[...], l1b_ref[...])


def _resident_spec(arr):
    zeros = (0,) * arr.ndim
    return pl.BlockSpec(arr.shape, lambda b, t: zeros, pipeline_mode=pl.Buffered(1))


def _row(v):
    return v.reshape(1, -1).astype(F32)


def _layer_call(x, apply_ln_in, params):
    bsz, seq, _ = x.shape
    tile = min(MIX_TILE, seq)
    nrows = MIX_ROWS if bsz % MIX_ROWS == 0 else 1
    n_tiles = seq // tile
    return pl.pallas_call(
        functools.partial(_layer_kernel, apply_ln_in),
        grid=(bsz // nrows, n_tiles + 1),
        in_specs=[pl.BlockSpec((nrows, tile, D_MODEL), lambda b, t: (b, jnp.minimum(t, n_tiles - 1), 0))]
        + [_resident_spec(p) for p in params],
        out_specs=pl.BlockSpec((nrows, tile, D_MODEL), lambda b, t: (b, jnp.maximum(t - 1, 0), 0)),
        out_shape=jax.ShapeDtypeStruct((bsz, seq, D_MODEL), F32),
        scratch_shapes=[pltpu.VMEM((nrows, tile + GDN_CONV_HIST, 2 * GDN_QK + GDN_V), F32),
                        pltpu.VMEM((nrows, GDN_HEADS, GDN_DK, GDN_DV), F32),
                        pltpu.VMEM((nrows, tile + CONV_HIST, CONV_DIM), F32),
                        pltpu.VMEM((nrows * tile, D_MODEL), F32)],
        compiler_params=pltpu.CompilerParams(dimension_semantics=("arbitrary", "arbitrary"),
                                             vmem_limit_bytes=VMEM_LIMIT_BYTES),
        name="layer",
    )(x, *params)


def kernel(x, ln_in_g, ln_in_b, w_in, b_gate, conv_dw_w, conv_dw_b, conv_ln_g, conv_ln_b, w_pa, gdn_conv_q, gdn_conv_k, gdn_conv_v, gdn_a_log, gdn_dt_bias, gdn_norm_g, w_pb, sgu_ln_g, sgu_ln_b, sgu_w_s, sgu_b_s, w_pc, w_o, ln1_g, ln1_b, w_ff1, b_ff1, w_ff2, b_ff2, ln2_g, ln2_b):
    o_a = 0
    o_q = o_a + 2 * CONV_DIM
    o_z = o_q + 2 * GDN_QK + GDN_V
    o_b = o_z + GDN_V
    o_uv = o_b + 2 * GDN_HEADS
    o_g = o_uv + 2 * SGU_DIM
    head_pad = LANES - 2 * GDN_HEADS
    lng, lnb = _row(ln_in_g), _row(ln_in_b)
    for l in range(DEPTH):
        w = w_in[l]
        wa = w[:, o_a:o_q].astype(BF16)
        wqkv = w[:, o_q:o_z].astype(BF16)
        wz = w[:, o_z:o_b].astype(BF16)
        wba = jnp.pad(w[:, o_b:o_uv], ((0, 0), (0, head_pad))).astype(BF16)
        wuv = w[:, o_uv:o_g].astype(BF16)
        wg = w[:, o_g:].astype(BF16)
        cw_qkv = jnp.concatenate([gdn_conv_q[l], gdn_conv_k[l], gdn_conv_v[l]], axis=1)
        alog = jnp.pad(gdn_a_log[l], (GDN_HEADS, head_pad)).reshape(1, LANES)
        dtb = jnp.pad(gdn_dt_bias[l], (GDN_HEADS, head_pad)).reshape(1, LANES)
        layer_params = (lng, lnb,
                        wqkv, wz, wba, cw_qkv, alog, dtb, _row(gdn_norm_g[l]),
                        wa, wuv, wg, _row(b_gate[l]), conv_dw_w[l], _row(conv_dw_b[l]),
                        _row(conv_ln_g[l]), _row(conv_ln_b[l]), w_pa[l].astype(BF16), w_pb[l].astype(BF16),
                        _row(sgu_ln_g[l]), _row(sgu_ln_b[l]), sgu_w_s[l], sgu_b_s[l].T, w_pc[l].astype(BF16),
                        w_o[l].astype(BF16), _row(ln1_g[l]), _row(ln1_b[l]),
                        w_ff1[l].astype(BF16), _row(b_ff1[l]), w_ff2[l].astype(BF16), _row(b_ff2[l]),
                        _row(ln2_g[l]), _row(ln2_b[l]))
        x = _layer_call(x, l == 0, layer_params)
    return x
```

```python
import functools
import math

import jax
import jax.numpy as jnp
from jax import lax
from jax.experimental import pallas as pl
from jax.experimental.pallas import tpu as pltpu

F32 = jnp.float32
BF16 = jnp.bfloat16

D_MODEL = 1024
DEPTH = 2
CONV_DIM = 512
CONV_WIDTH = 31
GDN_HEADS = 4
GDN_DK = 128
GDN_DV = 128
GDN_QK = GDN_HEADS * GDN_DK
GDN_V = GDN_HEADS * GDN_DV
GDN_CONV = 4
SGU_GROUPS = 4
SGU_GROUP_DIM = 128
SGU_DIM = SGU_GROUPS * SGU_GROUP_DIM
SGU_CHUNK = 128
D_FF = 4 * D_MODEL
PK_A = 0
PK_QKV = PK_A + 2 * CONV_DIM
PK_Z = PK_QKV + 2 * GDN_QK + GDN_V
PK_UV = PK_Z + GDN_V
PK_GATE = PK_UV + 2 * SGU_DIM
PK_END = PK_GATE + 3 * D_MODEL
DN_ALPHA = (2 * DEPTH) ** 0.25
LN_EPS = 1e-5
RMS_EPS = 1e-6

SUBLANES = 8
LANES = 128
VMEM_LIMIT_BYTES = 60 * 1024 * 1024

MIX_TILE = 128
MIX_ROWS = 2
FFN_CHUNK = 1024
MIX_COLS = 256
CONV_HIST = 32
INV_CAT = 64
GDN_CONV_HIST = 8


def _bdot(a, b):
    return jnp.dot(a, b, preferred_element_type=F32)


def _mm(a, b):
    return _bdot(a.astype(BF16), b.astype(BF16))


def _mm_nt(a, b):
    return lax.dot_general(a.astype(BF16), b.astype(BF16), (((1,), (1,)), ((), ())),
                           preferred_element_type=F32)


def _mm_tn(a, b):
    return lax.dot_general(a.astype(BF16), b.astype(BF16), (((0,), (0,)), ((), ())),
                           preferred_element_type=F32)


def _split3(a):
    hi = a.astype(BF16)
    r = a - hi.astype(F32)
    mid = r.astype(BF16)
    lo = (r - mid.astype(F32)).astype(BF16)
    return hi, mid, lo


def _mm_exact_lhs(a_bf16, b):
    out = None
    for part in _split3(b):
        term = _bdot(a_bf16, part)
        out = term if out is None else out + term
    return out


def _layer_norm(x, g, b):
    mu = jnp.mean(x, axis=-1, keepdims=True)
    xc = x - mu
    var = jnp.mean(xc * xc, axis=-1, keepdims=True)
    return xc * lax.rsqrt(var + LN_EPS) * g + b


def _sigmoid(x):
    return 1.0 / (1.0 + jnp.exp2(x * (-1.0 / math.log(2.0))))


def _silu(x):
    return x * _sigmoid(x)


def _gelu_tanh(x):
    return 0.5 * x * (1.0 + jnp.tanh(math.sqrt(2.0 / math.pi) * (x + 0.044715 * (x * x * x))))


def _softplus(x):
    return jnp.maximum(x, 0.0) + jnp.log1p(jnp.exp(-jnp.abs(x)))


def _run(jobs, count):
    for _ in range(min(count, len(jobs))):
        jobs.pop(0)()


def _unit_lower_inverses(lms, fillers):
    n = lms[0].shape[0]
    c = min(INV_CAT, n)
    g = n // c
    crow = lax.broadcasted_iota(jnp.int32, (c, n), 0)
    ccol = lax.broadcasted_iota(jnp.int32, (c, n), 1)
    cdist = crow ^ (ccol & (c - 1))
    frow = lax.broadcasted_iota(jnp.int32, (n, n), 0)
    fcol = lax.broadcasted_iota(jnp.int32, (n, n), 1)
    fdist = frow ^ fcol
    same_block = fdist < c
    block_mask = same_block.astype(BF16)

    def fold(m):
        out = m[(g - 1) * c:]
        for b in range(g - 2, -1, -1):
            out = jnp.where(ccol < (b + 1) * c, m[b * c:(b + 1) * c], out)
        return out

    def cat_dot(a, b):
        return _bdot(a, jnp.concatenate([b] * g, axis=0) * block_mask)

    lcs = [fold(lm) for lm in lms]
    lds = [jnp.where(cdist < SUBLANES, lc, 0.0) for lc in lcs]
    ldb = [ld.astype(BF16) for ld in lds]
    eye = (cdist == 0).astype(F32)
    xs = [eye - ld for ld in lds]
    pb = [cat_dot(b, b).astype(BF16) for b in ldb]
    xs = [x + cat_dot(x.astype(BF16), p) for x, p in zip(xs, pb)]
    pb = [cat_dot(p, p).astype(BF16) for p in pb]
    xs = [x + cat_dot(x.astype(BF16), p) for x, p in zip(xs, pb)]
    size = SUBLANES
    while size < c:
        band = jnp.logical_and(cdist >= size, cdist < 2 * size)
        eb = [jnp.where(band, lc, 0.0).astype(BF16) for lc in lcs]
        xb = [x.astype(BF16) for x in xs]
        ys = [cat_dot(e, x).astype(BF16) for e, x in zip(eb, xb)]
        _run(fillers, 1)
        xs = [x - cat_dot(b, y) for x, b, y in zip(xs, xb, ys)]
        size *= 2
    xs = [jnp.where(same_block, jnp.concatenate([x] * g, axis=0), 0.0) for x in xs]
    while size < n:
        band = jnp.logical_and(fdist >= size, fdist < 2 * size)
        eb = [jnp.where(band, lm, 0.0).astype(BF16) for lm in lms]
        xb = [x.astype(BF16) for x in xs]
        ys = [_bdot(e, x).astype(BF16) for e, x in zip(eb, xb)]
        _run(fillers, 1)
        xs = [x - _bdot(b, y) for x, b, y in zip(xs, xb, ys)]
        size *= 2
    return xs


def _layer_kernel(apply_ln_in, x_ref, lng_ref, lnb_ref,
                  win_ref, wba_ref, gcw_ref, alog_ref, dtb_ref, ng_ref,
                  bg_ref, cw_ref, cb_ref, clg_ref, clb_ref, pa_ref, pb_ref,
                  slg_ref, slb_ref, ws_ref, bst_ref, pc_ref, wo_ref, l1g_ref, l1b_ref,
                  w1_ref, b1_ref, w2_ref, b2_ref, l2g_ref, l2b_ref,
                  o_ref, cbuf_ref, s_ref, hbuf_ref, x1_ref):
    nr, ts = x_ref.shape[0], x_ref.shape[1]
    rows = [slice(r * ts, (r + 1) * ts) for r in range(nr)]
    t = pl.program_id(1)

    @pl.when(t == 0)
    def _():
        cbuf_ref[:, 0:GDN_CONV_HIST, :] = jnp.zeros((nr, GDN_CONV_HIST, cbuf_ref.shape[2]), F32)
        s_ref[...] = jnp.zeros(s_ref.shape, F32)
        hbuf_ref[:, 0:CONV_HIST, :] = jnp.zeros((nr, CONV_HIST, CONV_DIM), F32)
        x1_ref[...] = jnp.zeros(x1_ref.shape, F32)

    f_in = x1_ref[...]
    f_inb = f_in.astype(BF16)
    ffn_parts = []

    def ffn_job(ci):
        lo, hi = ci * FFN_CHUNK, (ci + 1) * FFN_CHUNK
        hid = jnp.square(jnp.maximum(_bdot(f_inb, w1_ref[:, lo:hi]) + b1_ref[:, lo:hi], 0.0))
        ffn_parts.append(_bdot(hid.astype(BF16), w2_ref[lo:hi, :]))

    f_jobs = [functools.partial(ffn_job, ci) for ci in range(D_FF // FFN_CHUNK)]

    x = x_ref[...].reshape(nr * ts, D_MODEL)
    if apply_ln_in:
        x = _layer_norm(x, lng_ref[...], lnb_ref[...])
    xb = x.astype(BF16)

    n_chunk = D_MODEL // MIX_COLS
    guv_cols, gate_cols = [], []

    def uv_job(i):
        lo = PK_UV + i * MIX_COLS
        guv_cols.append(_gelu_tanh(_bdot(xb, win_ref[:, lo:lo + MIX_COLS])))

    def gate_job(i):
        cols = slice(i * MIX_COLS, (i + 1) * MIX_COLS)
        lo = PK_GATE + i * MIX_COLS
        gate_cols.append(_sigmoid(_bdot(xb, win_ref[:, lo:lo + MIX_COLS]) + bg_ref[:, cols]))

    x_jobs = [functools.partial(uv_job, i) for i in range(n_chunk)]
    x_jobs += [functools.partial(gate_job, i) for i in range(3 * n_chunk)]

    pqkv = _bdot(xb, win_ref[:, PK_QKV:PK_Z])
    ba = _bdot(xb, wba_ref[...])
    z = _bdot(xb, win_ref[:, PK_Z:PK_UV])
    pa = _bdot(xb, win_ref[:, PK_A:PK_QKV])
    glu = pa[:, :CONV_DIM] * _sigmoid(pa[:, CONV_DIM:])
    for r in range(nr):
        hbuf_ref[r, CONV_HIST:CONV_HIST + ts, :] = glu[rows[r]]

    first = CONV_HIST - (CONV_WIDTH - 1)
    conv_strips = {}

    def conv_strip(r, i):
        lanes = slice(i * LANES, (i + 1) * LANES)
        acc = jnp.broadcast_to(cb_ref[:, lanes], (ts, LANES))
        for s in range(SUBLANES):
            taps = [j for j in range(CONV_WIDTH) if (first + j) % SUBLANES == s]
            nrow = ts if s == 0 else ts + SUBLANES
            part = None
            for j in taps:
                base = first + j - s
                term = cw_ref[j:j + 1, lanes] * hbuf_ref[r, base:base + nrow, lanes]
                part = term if part is None else part + term
            if s:
                part = pltpu.roll(part, nrow - s, axis=0)
            acc = acc + part[:ts, :]
        conv_strips[(r, i)] = acc

    v_jobs = [functools.partial(conv_strip, r, i) for i in range(CONV_DIM // LANES) for r in range(nr)]

    sconvs = []
    for r in range(nr):
        cbuf_ref[r, GDN_CONV_HIST:GDN_CONV_HIST + ts, :] = pqkv[rows[r]]
        sconv = gcw_ref[GDN_CONV - 1:GDN_CONV, :] * pqkv[rows[r]]
        for j in range(GDN_CONV - 1):
            off = GDN_CONV_HIST - (GDN_CONV - 1) + j
            sconv = sconv + gcw_ref[j:j + 1, :] * cbuf_ref[r, off:off + ts, :]
        cbuf_ref[r, 0:GDN_CONV_HIST, :] = cbuf_ref[r, ts:ts + GDN_CONV_HIST, :]
        sconvs.append(sconv)
        _run(v_jobs, 1)
    qkv = _silu(jnp.concatenate(sconvs, axis=0))
    _run(v_jobs, 1)

    beta_all = _sigmoid(ba)
    g_all = -jnp.exp(alog_ref[...]) * _softplus(ba + dtb_ref[...])
    row = lax.broadcasted_iota(jnp.int32, (ts, ts), 0)
    col = lax.broadcasted_iota(jnp.int32, (ts, ts), 1)
    ge = row >= col
    gt = row > col
    ltri = ge.astype(BF16)
    gam_alls = [_mm_exact_lhs(ltri, g_all[rows[r]]) for r in range(nr)]
    gam_all_ts = [g.T for g in gam_alls]

    heads = range(GDN_HEADS)
    chains = [(r, h) for r in range(nr) for h in heads]
    qn, kn = [], []
    for h in heads:
        q = qkv[:, h * GDN_DK:(h + 1) * GDN_DK]
        k = qkv[:, GDN_QK + h * GDN_DK:GDN_QK + (h + 1) * GDN_DK]
        qn.append(q * lax.rsqrt(jnp.sum(q * q, axis=-1, keepdims=True) + RMS_EPS) * (GDN_DK ** -0.5))
        kn.append(k * lax.rsqrt(jnp.sum(k * k, axis=-1, keepdims=True) + RMS_EPS))
        _run(v_jobs, 1)
    qs, ks, vs, betas, gams, decays, lms = {}, {}, {}, {}, {}, {}, {}
    for r, h in chains:
        qs[r, h] = qn[h][rows[r]]
        ks[r, h] = kn[h][rows[r]]
        vs[r, h] = qkv[rows[r], 2 * GDN_QK + h * GDN_DV:2 * GDN_QK + (h + 1) * GDN_DV]
        betas[r, h] = jnp.broadcast_to(beta_all[rows[r], h:h + 1], (ts, LANES))
        gams[r, h] = jnp.broadcast_to(gam_alls[r][:, GDN_HEADS + h:GDN_HEADS + h + 1], (ts, LANES))
    for r, h in chains:
        gam_i = jnp.concatenate([gams[r, h]] * (ts // LANES), axis=1)
        gam_j = jnp.broadcast_to(gam_all_ts[r][GDN_HEADS + h:GDN_HEADS + h + 1, :], (ts, ts))
        decays[r, h] = jnp.where(ge, jnp.exp(jnp.minimum(gam_i - gam_j, 0.0)), 0.0)
        beta_i = jnp.concatenate([betas[r, h]] * (ts // LANES), axis=1)
        lms[r, h] = jnp.where(gt, beta_i * _mm_nt(ks[r, h], ks[r, h]) * decays[r, h], 0.0)
        _run(v_jobs, 1)
    _run(v_jobs, len(v_jobs))
    for r in range(nr):
        hbuf_ref[r, 0:CONV_HIST, :] = hbuf_ref[r, ts:ts + CONV_HIST, :]
    tinv_list = _unit_lower_inverses([lms[c] for c in chains], x_jobs)
    tinvs = dict(zip(chains, tinv_list))
    _run(x_jobs, len(x_jobs))
    _run(f_jobs, 1)

    egams = {c: jnp.exp(gams[c]) for c in chains}
    uws = {c: _mm(tinvs[c], jnp.concatenate([betas[c] * vs[c], betas[c] * ks[c] * egams[c]], axis=1))
           for c in chains}
    _run(f_jobs, 1)
    guv = jnp.concatenate(guv_cols, axis=1)
    aqks = {c: jnp.where(ge, _mm_nt(qs[c], ks[c]) * decays[c], 0.0) for c in chains}
    glasts = {c: gams[c][ts - 1:ts, :] for c in chains}
    ss = {(r, h): s_ref[r, h] for r, h in chains}
    v_news = {c: uws[c][:, :GDN_DV] - _mm(uws[c][:, GDN_DV:], ss[c]) for c in chains}
    _run(f_jobs, 1)
    u_sgu = guv[:, :SGU_DIM]
    vb = _layer_norm(guv[:, SGU_DIM:], slg_ref[...], slb_ref[...]).astype(BF16)
    for r, h in chains:
        c = (r, h)
        s_ref[r, h] = ss[c] * jnp.exp(glasts[c]) + _mm_tn(ks[c] * jnp.exp(glasts[c] - gams[c]), v_news[c])
    gates = [jnp.concatenate(gate_cols[i * n_chunk:(i + 1) * n_chunk], axis=1) for i in range(3)]
    ogs = {}
    for r, h in chains:
        c = (r, h)
        o = _mm(qs[c] * egams[c], ss[c]) + _mm(aqks[c], v_news[c])
        ogs[c] = o * lax.rsqrt(jnp.mean(o * o, axis=-1, keepdims=True) + RMS_EPS) * ng_ref[...]
    _run(f_jobs, len(f_jobs))
    ffn = ffn_parts[0]
    for part in ffn_parts[1:]:
        ffn = ffn + part
    o_ref[...] = _layer_norm(DN_ALPHA * f_in + (ffn + b2_ref[...]), l2g_ref[...], l2b_ref[...]
                             ).reshape(nr, ts, D_MODEL)
    og = jnp.concatenate([jnp.concatenate([ogs[r, h] for h in heads], axis=1) for r in range(nr)], axis=0)
    yb = _mm(og * _silu(z), pb_ref[...])

    prow = lax.broadcasted_iota(jnp.int32, (SGU_CHUNK, SGU_CHUNK), 0)
    pcol = lax.broadcasted_iota(jnp.int32, (SGU_CHUNK, SGU_CHUNK), 1)
    w_causal = [jnp.where(prow >= pcol, ws_ref[g], 0.0).astype(BF16) for g in range(SGU_GROUPS)]
    chunks = []
    for ci in range(nr * ts // SGU_CHUNK):
        r0 = ci * SGU_CHUNK
        groups = []
        for g in range(SGU_GROUPS):
            vg = vb[r0:r0 + SGU_CHUNK, g * SGU_GROUP_DIM:(g + 1) * SGU_GROUP_DIM]
            groups.append(_bdot(w_causal[g], vg) + bst_ref[:, g:g + 1])
        chunks.append(jnp.concatenate(groups, axis=1))
    mixed = jnp.concatenate(chunks, axis=0)
    yc = _mm(u_sgu * mixed, pc_ref[...])

    conv = jnp.concatenate([jnp.concatenate([conv_strips[r, i] for i in range(CONV_DIM // LANES)], axis=1)
                            for r in range(nr)], axis=0)
    ya = _mm(_silu(_layer_norm(conv, clg_ref[...], clb_ref[...])), pa_ref[...])
    merged = gates[0] * ya + gates[1] * yb + gates[2] * yc
    m = _mm(merged, wo_ref[...])
    x1_ref[...] = _layer_norm(DN_ALPHA * x + m, l1g_ref[...], l1b_ref[...])


def _resident_spec(arr):
    zeros = (0,) * arr.ndim
    return pl.BlockSpec(arr.shape, lambda b, t: zeros, pipeline_mode=pl.Buffered(1))


def _row(v):
    return v.reshape(1, -1).astype(F32)


def _layer_call(x, apply_ln_in, params):
    bsz, seq, _ = x.shape
    tile = min(MIX_TILE, seq)
    nrows = MIX_ROWS if bsz % MIX_ROWS == 0 else 1
    n_tiles = seq // tile
    return pl.pallas_call(
        functools.partial(_layer_kernel, apply_ln_in),
        grid=(bsz // nrows, n_tiles + 1),
        in_specs=[pl.BlockSpec((nrows, tile, D_MODEL), lambda b, t: (b, jnp.minimum(t, n_tiles - 1), 0))]
        + [_resident_spec(p) for p in params],
        out_specs=pl.BlockSpec((nrows, tile, D_MODEL), lambda b, t: (b, jnp.maximum(t - 1, 0), 0)),
        out_shape=jax.ShapeDtypeStruct((bsz, seq, D_MODEL), F32),
        scratch_shapes=[pltpu.VMEM((nrows, tile + GDN_CONV_HIST, 2 * GDN_QK + GDN_V), F32),
                        pltpu.VMEM((nrows, GDN_HEADS, GDN_DK, GDN_DV), F32),
                        pltpu.VMEM((nrows, tile + CONV_HIST, CONV_DIM), F32),
                        pltpu.VMEM((nrows * tile, D_MODEL), F32)],
        compiler_params=pltpu.CompilerParams(dimension_semantics=("arbitrary", "arbitrary"),
                                             vmem_limit_bytes=VMEM_LIMIT_BYTES),
        name="layer",
    )(x, *params)


def kernel(x, ln_in_g, ln_in_b, w_in, b_gate, conv_dw_w, conv_dw_b, conv_ln_g, conv_ln_b, w_pa, gdn_conv_q, gdn_conv_k, gdn_conv_v, gdn_a_log, gdn_dt_bias, gdn_norm_g, w_pb, sgu_ln_g, sgu_ln_b, sgu_w_s, sgu_b_s, w_pc, w_o, ln1_g, ln1_b, w_ff1, b_ff1, w_ff2, b_ff2, ln2_g, ln2_b):
    o_b = PK_UV
    o_uv = o_b + 2 * GDN_HEADS
    head_pad = LANES - 2 * GDN_HEADS
    lng, lnb = _row(ln_in_g), _row(ln_in_b)
    for l in range(DEPTH):
        w = w_in[l]
        w_packed = jnp.concatenate([w[:, :o_b], w[:, o_uv:]], axis=1).astype(BF16)
        wba = jnp.pad(w[:, o_b:o_uv], ((0, 0), (0, head_pad))).astype(BF16)
        cw_qkv = jnp.concatenate([gdn_conv_q[l], gdn_conv_k[l], gdn_conv_v[l]], axis=1)
        alog = jnp.pad(gdn_a_log[l], (GDN_HEADS, head_pad)).reshape(1, LANES)
        dtb = jnp.pad(gdn_dt_bias[l], (GDN_HEADS, head_pad)).reshape(1, LANES)
        layer_params = (lng, lnb,
                        w_packed, wba, cw_qkv, alog, dtb, _row(gdn_norm_g[l]),
                        _row(b_gate[l]), conv_dw_w[l], _row(conv_dw_b[l]),
                        _row(conv_ln_g[l]), _row(conv_ln_b[l]), w_pa[l].astype(BF16), w_pb[l].astype(BF16),
                        _row(sgu_ln_g[l]), _row(sgu_ln_b[l]), sgu_w_s[l], sgu_b_s[l].T, w_pc[l].astype(BF16),
                        w_o[l].astype(BF16), _row(ln1_g[l]), _row(ln1_b[l]),
                        w_ff1[l].astype(BF16), _row(b_ff1[l]), w_ff2[l].astype(BF16), _row(b_ff2[l]),
                        _row(ln2_g[l]), _row(ln2_b[l]))
        x = _layer_call(x, l == 0, layer_params)
    return x
```

```python
import functools
import math

import jax
import jax.numpy as jnp
from jax import lax
from jax.experimental import pallas as pl
from jax.experimental.pallas import tpu as pltpu

F32 = jnp.float32
BF16 = jnp.bfloat16

D_MODEL = 1024
DEPTH = 2
CONV_DIM = 512
CONV_WIDTH = 31
GDN_HEADS = 4
GDN_DK = 128
GDN_DV = 128
GDN_QK = GDN_HEADS * GDN_DK
GDN_V = GDN_HEADS * GDN_DV
GDN_CONV = 4
SGU_GROUPS = 4
SGU_GROUP_DIM = 128
SGU_DIM = SGU_GROUPS * SGU_GROUP_DIM
SGU_CHUNK = 128
D_FF = 4 * D_MODEL
DN_ALPHA = (2 * DEPTH) ** 0.25
LN_EPS = 1e-5
RMS_EPS = 1e-6

SUBLANES = 8
LANES = 128
VMEM_LIMIT_BYTES = 60 * 1024 * 1024

MIX_TILE = 128
MIX_ROWS = 2
FFN_CHUNK = 1024
MIX_COLS = 256
CONV_HIST = 32
INV_CAT = 64
GDN_CONV_HIST = 8


def _bdot(a, b):
    return jnp.dot(a, b, preferred_element_type=F32)


def _mm(a, b):
    return _bdot(a.astype(BF16), b.astype(BF16))


def _mm_nt(a, b):
    return lax.dot_general(a.astype(BF16), b.astype(BF16), (((1,), (1,)), ((), ())),
                           preferred_element_type=F32)


def _mm_tn(a, b):
    return lax.dot_general(a.astype(BF16), b.astype(BF16), (((0,), (0,)), ((), ())),
                           preferred_element_type=F32)


def _split3(a):
    hi = a.astype(BF16)
    r = a - hi.astype(F32)
    mid = r.astype(BF16)
    lo = (r - mid.astype(F32)).astype(BF16)
    return hi, mid, lo


def _mm_exact_lhs(a_bf16, b):
    out = None
    for part in _split3(b):
        term = _bdot(a_bf16, part)
        out = term if out is None else out + term
    return out


def _layer_norm(x, g, b):
    mu = jnp.mean(x, axis=-1, keepdims=True)
    xc = x - mu
    var = jnp.mean(xc * xc, axis=-1, keepdims=True)
    return xc * lax.rsqrt(var + LN_EPS) * g + b


def _sigmoid(x):
    return 1.0 / (1.0 + jnp.exp2(x * (-1.0 / math.log(2.0))))


def _silu(x):
    return x * _sigmoid(x)


def _gelu_tanh(x):
    return 0.5 * x * (1.0 + jnp.tanh(math.sqrt(2.0 / math.pi) * (x + 0.044715 * (x * x * x))))


def _softplus(x):
    return jnp.maximum(x, 0.0) + jnp.log1p(jnp.exp(-jnp.abs(x)))


def _run(jobs, count):
    for _ in range(min(count, len(jobs))):
        jobs.pop(0)()


def _unit_lower_inverses(lms, fillers):
    n = lms[0].shape[0]
    c = min(INV_CAT, n)
    g = n // c
    crow = lax.broadcasted_iota(jnp.int32, (c, n), 0)
    ccol = lax.broadcasted_iota(jnp.int32, (c, n), 1)
    cdist = crow ^ (ccol & (c - 1))
    frow = lax.broadcasted_iota(jnp.int32, (n, n), 0)
    fcol = lax.broadcasted_iota(jnp.int32, (n, n), 1)
    fdist = frow ^ fcol
    same_block = fdist < c
    block_mask = same_block.astype(BF16)

    def fold(m):
        out = m[(g - 1) * c:]
        for b in range(g - 2, -1, -1):
            out = jnp.where(ccol < (b + 1) * c, m[b * c:(b + 1) * c], out)
        return out

    def cat_dot(a, b):
        return _bdot(a, jnp.concatenate([b] * g, axis=0) * block_mask)

    lcs = [fold(lm) for lm in lms]
    lds = [jnp.where(cdist < SUBLANES, lc, 0.0) for lc in lcs]
    ldb = [ld.astype(BF16) for ld in lds]
    eye = (cdist == 0).astype(F32)
    xs = [eye - ld for ld in lds]
    pb = [cat_dot(b, b).astype(BF16) for b in ldb]
    xs = [x + cat_dot(x.astype(BF16), p) for x, p in zip(xs, pb)]
    pb = [cat_dot(p, p).astype(BF16) for p in pb]
    xs = [x + cat_dot(x.astype(BF16), p) for x, p in zip(xs, pb)]
    size = SUBLANES
    while size < c:
        band = jnp.logical_and(cdist >= size, cdist < 2 * size)
        eb = [jnp.where(band, lc, 0.0).astype(BF16) for lc in lcs]
        xb = [x.astype(BF16) for x in xs]
        ys = [cat_dot(e, x).astype(BF16) for e, x in zip(eb, xb)]
        _run(fillers, 1)
        xs = [x - cat_dot(b, y) for x, b, y in zip(xs, xb, ys)]
        size *= 2
    xs = [jnp.where(same_block, jnp.concatenate([x] * g, axis=0), 0.0) for x in xs]
    while size < n:
        band = jnp.logical_and(fdist >= size, fdist < 2 * size)
        eb = [jnp.where(band, lm, 0.0).astype(BF16) for lm in lms]
        xb = [x.astype(BF16) for x in xs]
        ys = [_bdot(e, x).astype(BF16) for e, x in zip(eb, xb)]
        _run(fillers, 1)
        xs = [x - _bdot(b, y) for x, b, y in zip(xs, xb, ys)]
        size *= 2
    return xs


def _layer_kernel(apply_ln_in, x_ref, lng_ref, lnb_ref,
                  wqkv_ref, wz_ref, wba_ref, gcw_ref, alog_ref, dtb_ref, ng_ref,
                  wa_ref, wuv_ref, wg_ref, bg_ref, cw_ref, cb_ref, clg_ref, clb_ref, pa_ref, pb_ref,
                  slg_ref, slb_ref, ws_ref, bst_ref, pc_ref, wo_ref, l1g_ref, l1b_ref,
                  w1_ref, b1_ref, w2_ref, b2_ref, l2g_ref, l2b_ref,
                  o_ref, cbuf_ref, s_ref, hbuf_ref, xres_ref, mrg_ref):
    nr, ts = x_ref.shape[0], x_ref.shape[1]
    rows = [slice(r * ts, (r + 1) * ts) for r in range(nr)]
    t = pl.program_id(1)

    @pl.when(t == 0)
    def _():
        cbuf_ref[:, 0:GDN_CONV_HIST, :] = jnp.zeros((nr, GDN_CONV_HIST, cbuf_ref.shape[2]), F32)
        s_ref[...] = jnp.zeros(s_ref.shape, F32)
        hbuf_ref[:, 0:CONV_HIST, :] = jnp.zeros((nr, CONV_HIST, CONV_DIM), F32)
        xres_ref[...] = jnp.zeros(xres_ref.shape, F32)
        mrg_ref[...] = jnp.zeros(mrg_ref.shape, BF16)

    f_in = _layer_norm(DN_ALPHA * xres_ref[...] + _bdot(mrg_ref[...], wo_ref[...]), l1g_ref[...], l1b_ref[...])
    f_inb = f_in.astype(BF16)
    ffn_parts = []

    def ffn_job(ci):
        lo, hi = ci * FFN_CHUNK, (ci + 1) * FFN_CHUNK
        hid = jnp.square(jnp.maximum(_bdot(f_inb, w1_ref[:, lo:hi]) + b1_ref[:, lo:hi], 0.0))
        ffn_parts.append(_bdot(hid.astype(BF16), w2_ref[lo:hi, :]))

    f_jobs = [functools.partial(ffn_job, ci) for ci in range(D_FF // FFN_CHUNK)]

    x = x_ref[...].reshape(nr * ts, D_MODEL)
    if apply_ln_in:
        x = _layer_norm(x, lng_ref[...], lnb_ref[...])
    xb = x.astype(BF16)

    n_chunk = D_MODEL // MIX_COLS
    guv_cols, gate_cols = [], []

    def uv_job(i):
        guv_cols.append(_gelu_tanh(_bdot(xb, wuv_ref[:, i * MIX_COLS:(i + 1) * MIX_COLS])))

    def gate_job(i):
        cols = slice(i * MIX_COLS, (i + 1) * MIX_COLS)
        gate_cols.append(_sigmoid(_bdot(xb, wg_ref[:, cols]) + bg_ref[:, cols]))

    x_jobs = [functools.partial(uv_job, i) for i in range(n_chunk)]
    x_jobs += [functools.partial(gate_job, i) for i in range(3 * n_chunk)]

    pqkv = _bdot(xb, wqkv_ref[...])
    ba = _bdot(xb, wba_ref[...])
    z = _bdot(xb, wz_ref[...])
    pa = _bdot(xb, wa_ref[...])
    glu = pa[:, :CONV_DIM] * _sigmoid(pa[:, CONV_DIM:])
    for r in range(nr):
        hbuf_ref[r, CONV_HIST:CONV_HIST + ts, :] = glu[rows[r]]

    first = CONV_HIST - (CONV_WIDTH - 1)
    conv_strips = {}

    def conv_strip(r, i):
        lanes = slice(i * LANES, (i + 1) * LANES)
        acc = jnp.broadcast_to(cb_ref[:, lanes], (ts, LANES))
        for s in range(SUBLANES):
            taps = [j for j in range(CONV_WIDTH) if (first + j) % SUBLANES == s]
            nrow = ts if s == 0 else ts + SUBLANES
            part = None
            for j in taps:
                base = first + j - s
                term = cw_ref[j:j + 1, lanes] * hbuf_ref[r, base:base + nrow, lanes]
                part = term if part is None else part + term
            if s:
                part = pltpu.roll(part, nrow - s, axis=0)
            acc = acc + part[:ts, :]
        conv_strips[(r, i)] = acc

    v_jobs = [functools.partial(conv_strip, r, i) for i in range(CONV_DIM // LANES) for r in range(nr)]

    sconvs = []
    for r in range(nr):
        cbuf_ref[r, GDN_CONV_HIST:GDN_CONV_HIST + ts, :] = pqkv[rows[r]]
        sconv = gcw_ref[GDN_CONV - 1:GDN_CONV, :] * pqkv[rows[r]]
        for j in range(GDN_CONV - 1):
            off = GDN_CONV_HIST - (GDN_CONV - 1) + j
            sconv = sconv + gcw_ref[j:j + 1, :] * cbuf_ref[r, off:off + ts, :]
        cbuf_ref[r, 0:GDN_CONV_HIST, :] = cbuf_ref[r, ts:ts + GDN_CONV_HIST, :]
        sconvs.append(sconv)
        _run(v_jobs, 1)
    qkv = _silu(jnp.concatenate(sconvs, axis=0))
    _run(v_jobs, 1)

    beta_all = _sigmoid(ba)
    g_all = -jnp.exp(alog_ref[...]) * _softplus(ba + dtb_ref[...])
    row = lax.broadcasted_iota(jnp.int32, (ts, ts), 0)
    col = lax.broadcasted_iota(jnp.int32, (ts, ts), 1)
    ge = row >= col
    gt = row > col
    ltri = ge.astype(BF16)
    gam_alls = [_mm_exact_lhs(ltri, g_all[rows[r]]) for r in range(nr)]
    gam_all_ts = [g.T for g in gam_alls]

    heads = range(GDN_HEADS)
    chains = [(r, h) for r in range(nr) for h in heads]
    qn, kn = [], []
    for h in heads:
        q = qkv[:, h * GDN_DK:(h + 1) * GDN_DK]
        k = qkv[:, GDN_QK + h * GDN_DK:GDN_QK + (h + 1) * GDN_DK]
        qn.append(q * lax.rsqrt(jnp.sum(q * q, axis=-1, keepdims=True) + RMS_EPS) * (GDN_DK ** -0.5))
        kn.append(k * lax.rsqrt(jnp.sum(k * k, axis=-1, keepdims=True) + RMS_EPS))
        _run(v_jobs, 1)
    qs, ks, vs, betas, gams, decays, lms = {}, {}, {}, {}, {}, {}, {}
    for r, h in chains:
        qs[r, h] = qn[h][rows[r]]
        ks[r, h] = kn[h][rows[r]]
        vs[r, h] = qkv[rows[r], 2 * GDN_QK + h * GDN_DV:2 * GDN_QK + (h + 1) * GDN_DV]
        betas[r, h] = jnp.broadcast_to(beta_all[rows[r], h:h + 1], (ts, LANES))
        gams[r, h] = jnp.broadcast_to(gam_alls[r][:, GDN_HEADS + h:GDN_HEADS + h + 1], (ts, LANES))
    for r, h in chains:
        gam_i = jnp.concatenate([gams[r, h]] * (ts // LANES), axis=1)
        gam_j = jnp.broadcast_to(gam_all_ts[r][GDN_HEADS + h:GDN_HEADS + h + 1, :], (ts, ts))
        decays[r, h] = jnp.where(ge, jnp.exp(jnp.minimum(gam_i - gam_j, 0.0)), 0.0)
        beta_i = jnp.concatenate([betas[r, h]] * (ts // LANES), axis=1)
        lms[r, h] = jnp.where(gt, beta_i * _mm_nt(ks[r, h], ks[r, h]) * decays[r, h], 0.0)
        _run(v_jobs, 1)
    _run(v_jobs, len(v_jobs))
    for r in range(nr):
        hbuf_ref[r, 0:CONV_HIST, :] = hbuf_ref[r, ts:ts + CONV_HIST, :]
    tinv_list = _unit_lower_inverses([lms[c] for c in chains], x_jobs)
    tinvs = dict(zip(chains, tinv_list))
    _run(x_jobs, len(x_jobs))
    _run(f_jobs, 1)

    egams = {c: jnp.exp(gams[c]) for c in chains}
    uws = {c: _mm(tinvs[c], jnp.concatenate([betas[c] * vs[c], betas[c] * ks[c] * egams[c]], axis=1))
           for c in chains}
    _run(f_jobs, 1)
    guv = jnp.concatenate(guv_cols, axis=1)
    aqks = {c: jnp.where(ge, _mm_nt(qs[c], ks[c]) * decays[c], 0.0) for c in chains}
    glasts = {c: gams[c][ts - 1:ts, :] for c in chains}
    ss = {(r, h): s_ref[r, h] for r, h in chains}
    v_news = {c: uws[c][:, :GDN_DV] - _mm(uws[c][:, GDN_DV:], ss[c]) for c in chains}
    _run(f_jobs, 1)
    u_sgu = guv[:, :SGU_DIM]
    vb = _layer_norm(guv[:, SGU_DIM:], slg_ref[...], slb_ref[...]).astype(BF16)
    for r, h in chains:
        c = (r, h)
        s_ref[r, h] = ss[c] * jnp.exp(glasts[c]) + _mm_tn(ks[c] * jnp.exp(glasts[c] - gams[c]), v_news[c])
    gates = [jnp.concatenate(gate_cols[i * n_chunk:(i + 1) * n_chunk], axis=1) for i in range(3)]
    ogs = {}
    for r, h in chains:
        c = (r, h)
        o = _mm(qs[c] * egams[c], ss[c]) + _mm(aqks[c], v_news[c])
        ogs[c] = o * lax.rsqrt(jnp.mean(o * o, axis=-1, keepdims=True) + RMS_EPS) * ng_ref[...]
    _run(f_jobs, len(f_jobs))
    ffn = ffn_parts[0]
    for part in ffn_parts[1:]:
        ffn = ffn + part
    o_ref[...] = _layer_norm(DN_ALPHA * f_in + (ffn + b2_ref[...]), l2g_ref[...], l2b_ref[...]
                             ).reshape(nr, ts, D_MODEL)
    og = jnp.concatenate([jnp.concatenate([ogs[r, h] for h in heads], axis=1) for r in range(nr)], axis=0)
    yb = _mm(og * _silu(z), pb_ref[...])

    prow = lax.broadcasted_iota(jnp.int32, (SGU_CHUNK, SGU_CHUNK), 0)
    pcol = lax.broadcasted_iota(jnp.int32, (SGU_CHUNK, SGU_CHUNK), 1)
    w_causal = [jnp.where(prow >= pcol, ws_ref[g], 0.0).astype(BF16) for g in range(SGU_GROUPS)]
    chunks = []
    for ci in range(nr * ts // SGU_CHUNK):
        r0 = ci * SGU_CHUNK
        groups = []
        for g in range(SGU_GROUPS):
            vg = vb[r0:r0 + SGU_CHUNK, g * SGU_GROUP_DIM:(g + 1) * SGU_GROUP_DIM]
            groups.append(_bdot(w_causal[g], vg) + bst_ref[:, g:g + 1])
        chunks.append(jnp.concatenate(groups, axis=1))
    mixed = jnp.concatenate(chunks, axis=0)
    yc = _mm(u_sgu * mixed, pc_ref[...])

    conv = jnp.concatenate([jnp.concatenate([conv_strips[r, i] for i in range(CONV_DIM // LANES)], axis=1)
                            for r in range(nr)], axis=0)
    ya = _mm(_silu(_layer_norm(conv, clg_ref[...], clb_ref[...])), pa_ref[...])
    merged = gates[0] * ya + gates[1] * yb + gates[2] * yc
    mrg_ref[...] = merged.astype(BF16)
    xres_ref[...] = x


def _resident_spec(arr):
    zeros = (0,) * arr.ndim
    return pl.BlockSpec(arr.shape, lambda b, t: zeros, pipeline_mode=pl.Buffered(1))


def _row(v):
    return v.reshape(1, -1).astype(F32)


def _layer_call(x, apply_ln_in, params):
    bsz, seq, _ = x.shape
    tile = min(MIX_TILE, seq)
    nrows = MIX_ROWS if bsz % MIX_ROWS == 0 else 1
    n_tiles = seq // tile
    return pl.pallas_call(
        functools.partial(_layer_kernel, apply_ln_in),
        grid=(bsz // nrows, n_tiles + 1),
        in_specs=[pl.BlockSpec((nrows, tile, D_MODEL), lambda b, t: (b, jnp.minimum(t, n_tiles - 1), 0))]
        + [_resident_spec(p) for p in params],
        out_specs=pl.BlockSpec((nrows, tile, D_MODEL), lambda b, t: (b, jnp.maximum(t - 1, 0), 0)),
        out_shape=jax.ShapeDtypeStruct((bsz, seq, D_MODEL), F32),
        scratch_shapes=[pltpu.VMEM((nrows, tile + GDN_CONV_HIST, 2 * GDN_QK + GDN_V), F32),
                        pltpu.VMEM((nrows, GDN_HEADS, GDN_DK, GDN_DV), F32),
                        pltpu.VMEM((nrows, tile + CONV_HIST, CONV_DIM), F32),
                        pltpu.VMEM((nrows * tile, D_MODEL), F32),
                        pltpu.VMEM((nrows * tile, D_MODEL), BF16)],
        compiler_params=pltpu.CompilerParams(dimension_semantics=("arbitrary", "arbitrary"),
                                             vmem_limit_bytes=VMEM_LIMIT_BYTES),
        name="layer",
    )(x, *params)


def kernel(x, ln_in_g, ln_in_b, w_in, b_gate, conv_dw_w, conv_dw_b, conv_ln_g, conv_ln_b, w_pa, gdn_conv_q, gdn_conv_k, gdn_conv_v, gdn_a_log, gdn_dt_bias, gdn_norm_g, w_pb, sgu_ln_g, sgu_ln_b, sgu_w_s, sgu_b_s, w_pc, w_o, ln1_g, ln1_b, w_ff1, b_ff1, w_ff2, b_ff2, ln2_g, ln2_b):
    o_a = 0
    o_q = o_a + 2 * CONV_DIM
    o_z = o_q + 2 * GDN_QK + GDN_V
    o_b = o_z + GDN_V
    o_uv = o_b + 2 * GDN_HEADS
    o_g = o_uv + 2 * SGU_DIM
    head_pad = LANES - 2 * GDN_HEADS
    lng, lnb = _row(ln_in_g), _row(ln_in_b)
    for l in range(DEPTH):
        w = w_in[l]
        wa = w[:, o_a:o_q].astype(BF16)
        wqkv = w[:, o_q:o_z].astype(BF16)
        wz = w[:, o_z:o_b].astype(BF16)
        wba = jnp.pad(w[:, o_b:o_uv], ((0, 0), (0, head_pad))).astype(BF16)
        wuv = w[:, o_uv:o_g].astype(BF16)
        wg = w[:, o_g:].astype(BF16)
        cw_qkv = jnp.concatenate([gdn_conv_q[l], gdn_conv_k[l], gdn_conv_v[l]], axis=1)
        alog = jnp.pad(gdn_a_log[l], (GDN_HEADS, head_pad)).reshape(1, LANES)
        dtb = jnp.pad(gdn_dt_bias[l], (GDN_HEADS, head_pad)).reshape(1, LANES)
        layer_params = (lng, lnb,
                        wqkv, wz, wba, cw_qkv, alog, dtb, _row(gdn_norm_g[l]),
                        wa, wuv, wg, _row(b_gate[l]), conv_dw_w[l], _row(conv_dw_b[l]),
                        _row(conv_ln_g[l]), _row(conv_ln_b[l]), w_pa[l].astype(BF16), w_pb[l].astype(BF16),
                        _row(sgu_ln_g[l]), _row(sgu_ln_b[l]), sgu_w_s[l], sgu_b_s[l].T, w_pc[l].astype(BF16),
                        w_o[l].astype(BF16), _row(ln1_g[l]), _row(ln1_b[l]),
                        w_ff1[l].astype(BF16), _row(b_ff1[l]), w_ff2[l].astype(BF16), _row(b_ff2[l]),
                        _row(ln2_g[l]), _row(ln2_b[l]))
        x = _layer_call(x, l == 0, layer_params)
    return x
```
